```python
import jax, jax.numpy as jnp
from jax import lax
import numpy as np

D_MODEL = 1024
BATCH = 32
SEQ = 2048
DEPTH = 2

GRID_W = 64
ATTN_HEADS = 16
ATTN_KV_HEADS = 4
ATTN_HEAD_DIM = 64
ATTN_GROUP = ATTN_HEADS // ATTN_KV_HEADS
ATTN_WIDTH = ATTN_HEADS * ATTN_HEAD_DIM
ATTN_KV_WIDTH = ATTN_KV_HEADS * ATTN_HEAD_DIM
ROPE_AXIS_DIM = ATTN_HEAD_DIM // 2
ROPE_THETA = 10000.0
Q_BLOCK = 128
HGRN_EXPAND = 128
HGRN_HEADS = D_MODEL // HGRN_EXPAND
HGRN_HEAD_V = 128
HGRN_KEY_WIDTH = HGRN_HEADS * HGRN_EXPAND
HGRN_VAL_WIDTH = HGRN_HEADS * HGRN_HEAD_V
HGRN_CHUNK = 32
N_DIRECTIONS = 2
NORM_EPS = 1e-6

SPLIT_SIZES = (ATTN_WIDTH, ATTN_KV_WIDTH, ATTN_KV_WIDTH, ATTN_WIDTH,
               HGRN_KEY_WIDTH, HGRN_KEY_WIDTH, HGRN_KEY_WIDTH, HGRN_VAL_WIDTH, HGRN_VAL_WIDTH,
               D_MODEL, D_MODEL)
N_IN = sum(SPLIT_SIZES)

kernel_name = "hybrid_gqa_axialrope_hgrn2_bidir"


def _split_indices():
    return [int(i) for i in np.cumsum(SPLIT_SIZES)[:-1]]


def _rms_norm(x, w):
    xf = x.astype(jnp.float32)
    y = xf * lax.rsqrt(jnp.mean(xf * xf, axis=-1, keepdims=True) + NORM_EPS)
    return (y * w.astype(jnp.float32)).astype(x.dtype)


def _axial_rope(seq_len):
    rows = seq_len // GRID_W
    row = jnp.repeat(jnp.arange(rows), GRID_W).astype(jnp.float32)
    col = jnp.tile(jnp.arange(GRID_W), rows).astype(jnp.float32)
    inv_freq = ROPE_THETA ** (-jnp.arange(0, ROPE_AXIS_DIM, 2, dtype=jnp.float32) / ROPE_AXIS_DIM)
    ang_r = row[:, None] * inv_freq
    ang_c = col[:, None] * inv_freq
    return jnp.cos(ang_r), jnp.sin(ang_r), jnp.cos(ang_c), jnp.sin(ang_c)


def _rotate(seg, c, s):
    half = seg.shape[-1] // 2
    x1, x2 = seg[..., :half], seg[..., half:]
    return jnp.concatenate([x1 * c - x2 * s, x2 * c + x1 * s], axis=-1)


def _apply_axial_rope(x, rope):
    cr, sr, cc, sc = [t[None, :, None, :] for t in rope]
    xf = x.astype(jnp.float32)
    out = jnp.concatenate([_rotate(xf[..., :ROPE_AXIS_DIM], cr, sr),
                           _rotate(xf[..., ROPE_AXIS_DIM:], cc, sc)], axis=-1)
    return out.astype(x.dtype)


def _blocked_attention(q, k, v):
    B, S, _, hd = q.shape
    nb = S // Q_BLOCK
    qb = q.reshape(B, nb, Q_BLOCK, ATTN_KV_HEADS, ATTN_GROUP, hd).transpose(1, 0, 3, 4, 2, 5)
    kt = k.transpose(0, 2, 1, 3)
    vt = v.transpose(0, 2, 1, 3)
    scale = hd ** -0.5

    def one_block(q_blk):
        s = jnp.einsum('bkgqd,bksd->bkgqs', q_blk, kt).astype(jnp.float32) * scale
        p = jax.nn.softmax(s, axis=-1).astype(vt.dtype)
        return jnp.einsum('bkgqs,bksd->bkgqd', p, vt)

    o = lax.map(one_block, qb)
    return o.transpose(1, 0, 4, 2, 3, 5).reshape(B, S, ATTN_WIDTH)


def _attention_branch(aq, ak, av, q_norm_w, k_norm_w, rope):
    B, S, _ = aq.shape
    q = aq.reshape(B, S, ATTN_HEADS, ATTN_HEAD_DIM)
    k = ak.reshape(B, S, ATTN_KV_HEADS, ATTN_HEAD_DIM)
    v = av.reshape(B, S, ATTN_KV_HEADS, ATTN_HEAD_DIM)
    q = _apply_axial_rope(_rms_norm(q, q_norm_w), rope)
    k = _apply_axial_rope(_rms_norm(k, k_norm_w), rope)
    return _blocked_attention(q, k, v)


def _hgrn2_chunk_scan(q, k, v, log_f):
    B, H, S, DK = q.shape
    DV = v.shape[-1]
    n = S // HGRN_CHUNK

    def to_chunks(a):
        return a.reshape(B, H, n, HGRN_CHUNK, a.shape[-1]).transpose(2, 0, 1, 3, 4)

    mask = jnp.tril(jnp.ones((HGRN_CHUNK, HGRN_CHUNK), dtype=bool))[None, None, :, :, None]

    def step(state, inp):
        q_c, k_c, v_c, g_c = inp
        b = jnp.cumsum(g_c, axis=2)
        diff = b[:, :, :, None, :] - b[:, :, None, :, :]
        decay = jnp.exp(jnp.where(mask, diff, -jnp.inf))
        scores = jnp.einsum('bhtk,bhsk,bhtsk->bhts', q_c, k_c, decay)
        o = jnp.einsum('bhts,bhsv->bhtv', scores, v_c) \
            + jnp.einsum('bhtk,bhkv->bhtv', q_c * jnp.exp(b), state)
        b_last = b[:, :, -1:, :]
        state = jnp.exp(b_last[:, :, 0, :])[..., None] * state \
            + jnp.einsum('bhsk,bhsv->bhkv', k_c * jnp.exp(b_last - b), v_c)
        return state, o

    state0 = jnp.zeros((B, H, DK, DV), jnp.float32)
    _, o = lax.scan(step, state0, (to_chunks(q), to_chunks(k), to_chunks(v), to_chunks(log_f)))
    return o.transpose(1, 2, 0, 3, 4).reshape(B, H, S, DV)


def _hgrn2_branch(hq, hf_fwd, hf_bwd, hi, lb_layer, g_norm_w):
    B, S, _ = hq.shape

    def heads(a, d):
        return a.reshape(B, S, HGRN_HEADS, d).transpose(0, 2, 1, 3).astype(jnp.float32)

    q = jax.nn.silu(heads(hq, HGRN_EXPAND)) * (HGRN_EXPAND ** -0.5)
    v = heads(hi, HGRN_HEAD_V)

    def direction(z_raw, lb, backward):
        z = heads(z_raw, HGRN_EXPAND)
        lb = lb.reshape(1, HGRN_HEADS, 1, HGRN_EXPAND)
        log_f = jnp.logaddexp(jnp.log(lb), jnp.log1p(-lb) + jax.nn.log_sigmoid(z))
        k = -jnp.expm1(log_f)
        if backward:
            o = _hgrn2_chunk_scan(q[:, :, ::-1], k[:, :, ::-1], v[:, :, ::-1], log_f[:, :, ::-1])
            return o[:, :, ::-1]
        return _hgrn2_chunk_scan(q, k, v, log_f)

    o = direction(hf_fwd, lb_layer[0], False) + direction(hf_bwd, lb_layer[1], True)
    o = _rms_norm(o, g_norm_w)
    return o.transpose(0, 2, 1, 3).reshape(B, S, HGRN_VAL_WIDTH).astype(hq.dtype)


def setup_inputs(seed: int = 0) -> dict:
    key = jax.random.key(seed)
    ks = jax.random.split(key, 12)
    f32 = jnp.float32
    x = jax.random.normal(ks[0], (BATCH, SEQ, D_MODEL), f32)
    w_in = jax.random.normal(ks[1], (DEPTH, D_MODEL, N_IN), f32) * D_MODEL ** -0.5
    norm_w = 1.0 + 0.02 * jax.random.normal(ks[2], (DEPTH, D_MODEL), f32)
    q_norm_w = 1.0 + 0.02 * jax.random.normal(ks[3], (DEPTH, ATTN_HEAD_DIM), f32)
    k_norm_w = 1.0 + 0.02 * jax.random.normal(ks[4], (DEPTH, ATTN_HEAD_DIM), f32)
    hgrn_lower_bounds = 0.1 * jax.random.normal(ks[5], (N_DIRECTIONS, DEPTH, HGRN_KEY_WIDTH), f32)
    hgrn_norm_w = 1.0 + 0.02 * jax.random.normal(ks[6], (DEPTH, HGRN_HEAD_V), f32)
    w_branch_attn = jax.random.normal(ks[7], (DEPTH, ATTN_WIDTH, D_MODEL), f32) * ATTN_WIDTH ** -0.5
    w_branch_hgrn = jax.random.normal(ks[8], (DEPTH, HGRN_VAL_WIDTH, D_MODEL), f32) * HGRN_VAL_WIDTH ** -0.5
    w_out = jax.random.normal(ks[9], (DEPTH, D_MODEL, D_MODEL), f32) * D_MODEL ** -0.5
    final_norm_w = 1.0 + 0.02 * jax.random.normal(ks[10], (D_MODEL,), f32)
    return {"x": x, "w_in": w_in, "norm_w": norm_w, "q_norm_w": q_norm_w, "k_norm_w": k_norm_w,
            "hgrn_lower_bounds": hgrn_lower_bounds, "hgrn_norm_w": hgrn_norm_w,
            "w_branch_attn": w_branch_attn, "w_branch_hgrn": w_branch_hgrn, "w_out": w_out,
            "final_norm_w": final_norm_w}


def reference(x, w_in, norm_w, q_norm_w, k_norm_w, hgrn_lower_bounds, hgrn_norm_w,
              w_branch_attn, w_branch_hgrn, w_out, final_norm_w):
    S = x.shape[1]
    rope = _axial_rope(S)
    lb = jnp.cumsum(jax.nn.softmax(hgrn_lower_bounds.astype(jnp.float32), axis=1), axis=1)
    lb = lb - lb[:, :1]
    split_at = _split_indices()
    for layer in range(DEPTH):
        h = _rms_norm(x, norm_w[layer])
        proj = jnp.einsum('bsd,de->bse', h, w_in[layer])
        (aq, ak, av, a_gate, hq, hf_fwd, hf_bwd, hi, h_gate, m_attn, m_hgrn) = jnp.split(proj, split_at, axis=-1)
        y_attn = _attention_branch(aq, ak, av, q_norm_w[layer], k_norm_w[layer], rope) * jax.nn.silu(a_gate)
        y_hgrn = _hgrn2_branch(hq, hf_fwd, hf_bwd, hi, lb[:, layer], hgrn_norm_w[layer]) * jax.nn.silu(h_gate)
        merged = jax.nn.sigmoid(m_attn) * jnp.einsum('bse,ed->bsd', y_attn, w_branch_attn[layer]) \
            + jax.nn.sigmoid(m_hgrn) * jnp.einsum('bse,ed->bsd', y_hgrn, w_branch_hgrn[layer])
        x = x + jnp.einsum('bsd,de->bse', merged, w_out[layer])
    return _rms_norm(x, final_norm_w)
```

```python
import functools
import math

import jax
import jax.numpy as jnp
from jax import lax
from jax.experimental import pallas as pl
from jax.experimental.pallas import tpu as pltpu

F32 = jnp.float32
BF16 = jnp.bfloat16

GRID_W = 64
ATTN_HEADS = 16
ATTN_KV_HEADS = 4
ATTN_HEAD_DIM = 64
ATTN_GROUP = ATTN_HEADS // ATTN_KV_HEADS
ROPE_AXIS_DIM = ATTN_HEAD_DIM // 2
ROPE_THETA = 10000.0
HGRN_HEADS = 8
HGRN_DK = 128
HGRN_DV = 128
NORM_EPS = 1e-6

GROUP_Q = ATTN_GROUP * ATTN_HEAD_DIM
GROUP_KV = 2 * ATTN_HEAD_DIM
GROUP_COLS = GROUP_Q + GROUP_KV + GROUP_Q
HEAD_COLS = 5 * HGRN_DK

Q_TILE = 256
HGRN_CHUNK = 64
HGRN_MID = HGRN_CHUNK // 2 - 1
HGRN_SAFE_EXP = 64.0
MERGE_TILE = 512

VMEM_LIMIT = 56 * 1024 * 1024


def _dot(a, b):
    return jnp.dot(a, b, preferred_element_type=F32)


def _dot_nt(a, b):
    return lax.dot_general(a, b, (((1,), (1,)), ((), ())), preferred_element_type=F32)


def _dot_tn(a, b):
    return lax.dot_general(a, b, (((0,), (0,)), ((), ())), preferred_element_type=F32)


def _silu(x):
    return x * jax.nn.sigmoid(x)


def _split3(x):
    hi = x.astype(BF16)
    r1 = x - hi.astype(F32)
    mid = r1.astype(BF16)
    lo = (r1 - mid.astype(F32)).astype(BF16)
    return hi, mid, lo


def _split2(x):
    hi = x.astype(BF16)
    lo = (x - hi.astype(F32)).astype(BF16)
    return hi, lo


def _prenorm_kernel(x_ref, w_ref, h_ref):
    x = x_ref[...]
    ms = jnp.mean(x * x, axis=-1, keepdims=True)
    h_ref[...] = (x * lax.rsqrt(ms + NORM_EPS) * w_ref[...]).astype(BF16)


def _prenorm(x2, w):
    n, d = x2.shape
    tm = MERGE_TILE
    return pl.pallas_call(
        _prenorm_kernel,
        grid=(n // tm,),
        in_specs=[pl.BlockSpec((tm, d), lambda i: (i, 0)), pl.BlockSpec((1, d), lambda i: (0, 0))],
        out_specs=pl.BlockSpec((tm, d), lambda i: (i, 0)),
        out_shape=jax.ShapeDtypeStruct((n, d), BF16),
        compiler_params=pltpu.CompilerParams(dimension_semantics=("arbitrary",)),
    )(x2, w.reshape(1, d))


def _rope_partner(x):
    width = x.shape[-1]
    lane = lax.broadcasted_iota(jnp.int32, x.shape, 1)
    first = (lane & (ROPE_AXIS_DIM - 1)) < (ROPE_AXIS_DIM // 2)
    return jnp.where(first, pltpu.roll(x, width - ROPE_AXIS_DIM // 2, 1), pltpu.roll(x, ROPE_AXIS_DIM // 2, 1))


def _attn_kernel(h_ref, w_ref, cos_ref, sin_ref, qw_ref, kw_ref, out_ref, qt_ref, k_ref, vt_ref, gate_ref):
    seq = h_ref.shape[1]
    n_tiles = seq // Q_TILE
    hd = ATTN_HEAD_DIM

    proj = _dot(h_ref[0], w_ref[0])
    q = proj[:, :GROUP_Q]
    kv = proj[:, GROUP_Q:GROUP_Q + GROUP_KV]
    gate_ref[...] = _silu(proj[:, GROUP_Q + GROUP_KV:])

    cos = cos_ref[...]
    sin = sin_ref[...]

    r = lax.broadcasted_iota(jnp.int32, (GROUP_Q, GROUP_Q), 0)
    c = lax.broadcasted_iota(jnp.int32, (GROUP_Q, GROUP_Q), 1)
    same_head = jnp.where((r ^ c) < hd, 1.0, 0.0).astype(BF16)
    hi, lo = _split2(q * q)
    ssq = _dot(hi, same_head) + _dot(lo, same_head)
    q_scale = qw_ref[...] * (hd ** -0.5 * math.log2(math.e))
    qn = q * lax.rsqrt(ssq * (1.0 / hd) + NORM_EPS) * q_scale
    cos_q = jnp.concatenate([cos, cos], axis=1)
    sin_q = jnp.concatenate([sin, sin], axis=1)
    qr = qn * cos_q + _rope_partner(qn) * sin_q
    qt = qr.T.astype(BF16)
    for i in range(n_tiles):
        qt_ref[i] = qt[:, i * Q_TILE:(i + 1) * Q_TILE]

    lane = lax.broadcasted_iota(jnp.int32, (seq, GROUP_KV), 1)
    is_k = lane < hd
    rk = lax.broadcasted_iota(jnp.int32, (GROUP_KV, GROUP_KV), 0)
    ck = lax.broadcasted_iota(jnp.int32, (GROUP_KV, GROUP_KV), 1)
    k_block = jnp.where((rk < hd) & (ck < hd), 1.0, 0.0).astype(BF16)
    hi, lo = _split2(kv * kv)
    ssk = _dot(hi, k_block) + _dot(lo, k_block)
    kvn = kv * jnp.where(is_k, lax.rsqrt(ssk * (1.0 / hd) + NORM_EPS) * kw_ref[...], 1.0)
    kvr = kvn * jnp.where(is_k, cos, 1.0) + _rope_partner(kvn) * jnp.where(is_k, sin, 0.0)
    k_ref[...] = kvr[:, :hd].astype(BF16)
    vt_ref[...] = kvr.T[hd:, :].astype(BF16)

    def tile(i, carry):
        col = pl.multiple_of(i * Q_TILE, Q_TILE)
        q_tile = qt_ref[i]
        outs = []
        for j in range(ATTN_GROUP):
            st = _dot(k_ref[...], q_tile[j * hd:(j + 1) * hd, :])
            m = jnp.max(st, axis=0, keepdims=True)
            p = jnp.exp2(st - m)
            l = jnp.sum(p, axis=0, keepdims=True)
            ot = _dot(vt_ref[...], p.astype(BF16))
            outs.append(ot / l)
        o = jnp.concatenate(outs, axis=0).T
        out_ref[0, pl.ds(col, Q_TILE), :] = (o * gate_ref[pl.ds(col, Q_TILE), :]).astype(BF16)
        return carry

    lax.fori_loop(0, n_tiles, tile, 0)


def _attention(h, w_a, cos, sin, qw, kw):
    b, s, d = h.shape
    g = w_a.shape[0]
    return pl.pallas_call(
        _attn_kernel,
        grid=(b, g),
        in_specs=[
            pl.BlockSpec((1, s, d), lambda i, j: (i, 0, 0)),
            pl.BlockSpec((1, d, GROUP_COLS), lambda i, j: (j, 0, 0)),
            pl.BlockSpec((s, GROUP_KV), lambda i, j: (0, 0)),
            pl.BlockSpec((s, GROUP_KV), lambda i, j: (0, 0)),
            pl.BlockSpec((1, GROUP_Q), lambda i, j: (0, 0)),
            pl.BlockSpec((1, GROUP_KV), lambda i, j: (0, 0)),
        ],
        out_specs=pl.BlockSpec((1, s, GROUP_Q), lambda i, j: (i, 0, j)),
        out_shape=jax.ShapeDtypeStruct((b, s, g * GROUP_Q), BF16),
        scratch_shapes=[
            pltpu.VMEM((s // Q_TILE, GROUP_Q, Q_TILE), BF16),
            pltpu.VMEM((s, ATTN_HEAD_DIM), BF16),
            pltpu.VMEM((ATTN_HEAD_DIM, s), BF16),
            pltpu.VMEM((s, GROUP_Q), F32),
        ],
        compiler_params=pltpu.CompilerParams(
            dimension_semantics=("arbitrary", "arbitrary"), vmem_limit_bytes=VMEM_LIMIT),
    )(h, w_a, cos, sin, qw, kw)


def _log_sigmoid(z):
    return jnp.minimum(z, 0.0) - jnp.log1p(jnp.exp(-jnp.abs(z)))


def _logaddexp(a, b):
    mx = jnp.maximum(a, b)
    mn = jnp.minimum(a, b)
    return mx + jnp.log1p(jnp.exp(mn - mx))


def _hgrn_kernel(layer, h_ref, w_ref, lbp_ref, gw_ref, out_ref, q_ref, v_ref, k_ref, b_ref, oi_ref):
    seq = h_ref.shape[1]
    ch = HGRN_CHUNK
    n_chunks = seq // ch
    dk = HGRN_DK

    proj = _dot(h_ref[0], w_ref[0])
    q = _silu(proj[:, :dk]) * (dk ** -0.5)
    v = proj[:, 3 * dk:4 * dk]
    gate = _silu(proj[:, 4 * dk:])
    q_ref[...] = q
    v_ref[...] = v
    vb = v.astype(BF16)

    row = lax.broadcasted_iota(jnp.int32, (ch, ch), 0)
    colm = lax.broadcasted_iota(jnp.int32, (ch, ch), 1)

    o_inter = jnp.zeros((seq, HGRN_DV), F32)
    for d in range(2):
        backward = d == 1
        p = lbp_ref[d]
        depth = p.shape[0]
        pm = p[0:1]
        for i in range(1, depth):
            pm = jnp.maximum(pm, p[i:i + 1])
        ex = [jnp.exp(p[i:i + 1] - pm) for i in range(depth)]
        tot = ex[0]
        for i in range(1, depth):
            tot = tot + ex[i]
        cum = ex[0] / tot
        first = cum
        for i in range(1, layer + 1):
            cum = cum + ex[i] / tot
        lb = cum - first

        z = proj[:, (1 + d) * dk:(2 + d) * dk]
        g = _logaddexp(jnp.log(lb), jnp.log1p(-lb) + _log_sigmoid(z))
        k = (1.0 - lb) * jax.nn.sigmoid(-z)
        k_ref[d] = k

        tri = (colm >= row) if backward else (colm <= row)
        tri_b = jnp.where(tri, 1.0, 0.0).astype(BF16)
        g3 = jnp.concatenate(_split3(g), axis=1)
        edge = 0 if backward else ch - 1
        mid = ch - 1 - HGRN_MID if backward else HGRN_MID

        viol = jnp.zeros((ch, dk), F32)
        gs, es, qhs, ois = [], [], [], []
        for c in range(n_chunks):
            sl = slice(c * ch, (c + 1) * ch)
            cs = _dot(tri_b, g3[sl])
            bc = (cs[:, :dk] + cs[:, dk:2 * dk]) + cs[:, 2 * dk:]
            b_ref[d, sl, :] = bc
            ref = bc[mid:mid + 1]
            viol = jnp.maximum(viol, jnp.abs(bc - ref))
            qc = q[sl]
            kc = k[sl]
            qe = (qc * jnp.exp(bc - ref)).astype(BF16)
            ke = (kc * jnp.exp(ref - bc)).astype(BF16)
            a = jnp.where(tri, _dot_nt(qe, ke), 0.0).astype(BF16)
            ois.append(_dot(a, vb[sl]))
            b_edge = bc[edge:edge + 1]
            qhs.append((qc * jnp.exp(bc)).astype(BF16))
            kd = (kc * jnp.exp(b_edge - bc)).astype(BF16)
            gs.append(_dot_tn(vb[sl], kd))
            es.append(jnp.exp(b_edge))
        oi_ref[d] = jnp.concatenate(ois, axis=0)

        state = jnp.zeros((HGRN_DV, dk), F32)
        order = range(n_chunks - 1, -1, -1) if backward else range(n_chunks)
        inter = [None] * n_chunks
        for c in order:
            inter[c] = _dot_nt(qhs[c], state.astype(BF16))
            state = state * es[c] + gs[c]
        o_inter = o_inter + jnp.concatenate(inter, axis=0)

        @pl.when(jnp.max(viol) > HGRN_SAFE_EXP)
        def _():
            def one_row(t, carry):
                t0 = pl.multiple_of((t // ch) * ch, ch)
                bc = b_ref[d, pl.ds(t0, ch), :]
                bt = b_ref[d, pl.ds(t, 1), :]
                s_idx = t0 + lax.broadcasted_iota(jnp.int32, (ch, 1), 0)
                valid = (s_idx >= t) if backward else (s_idx <= t)
                dec = jnp.where(valid, jnp.exp(jnp.minimum(bt - bc, 0.0)), 0.0)
                w = dec * (q_ref[pl.ds(t, 1), :] * k_ref[d, pl.ds(t0, ch), :])
                sc = jnp.sum(w, axis=1, keepdims=True)
                oi_ref[d, pl.ds(t, 1), :] = jnp.sum(sc * v_ref[pl.ds(t0, ch), :], axis=0, keepdims=True)
                return carry

            lax.fori_loop(0, seq, one_row, 0)

    o = o_inter + oi_ref[0] + oi_ref[1]
    ms = jnp.mean(o * o, axis=-1, keepdims=True)
    out_ref[0] = (o * lax.rsqrt(ms + NORM_EPS) * gw_ref[...] * gate).astype(BF16)


def _hgrn(h, w_h, lb_params, gw, layer):
    b, s, d = h.shape
    nh = w_h.shape[0]
    ndir, depth, _ = lb_params.shape
    return pl.pallas_call(
        functools.partial(_hgrn_kernel, layer),
        grid=(b, nh),
        in_specs=[
            pl.BlockSpec((1, s, d), lambda i, j: (i, 0, 0)),
            pl.BlockSpec((1, d, HEAD_COLS), lambda i, j: (j, 0, 0)),
            pl.BlockSpec((ndir, depth, HGRN_DK), lambda i, j: (0, 0, j)),
            pl.BlockSpec((1, HGRN_DV), lambda i, j: (0, 0)),
        ],
        out_specs=pl.BlockSpec((1, s, HGRN_DV), lambda i, j: (i, 0, j)),
        out_shape=jax.ShapeDtypeStruct((b, s, nh * HGRN_DV), BF16),
        scratch_shapes=[
            pltpu.VMEM((s, HGRN_DK), F32),
            pltpu.VMEM((s, HGRN_DV), F32),
            pltpu.VMEM((2, s, HGRN_DK), F32),
            pltpu.VMEM((2, s, HGRN_DK), F32),
            pltpu.VMEM((2, s, HGRN_DV), F32),
        ],
        compiler_params=pltpu.CompilerParams(
            dimension_semantics=("arbitrary", "arbitrary"), vmem_limit_bytes=VMEM_LIMIT),
    )(h, w_h, lb_params, gw)


def _merge_kernel(last, x_ref, h_ref, ya_ref, yh_ref, wm_ref, wba_ref, wbh_ref, wo_ref, nw_ref, *out_refs):
    d = x_ref.shape[1]
    m = _dot(h_ref[...], wm_ref[...])
    merged = jax.nn.sigmoid(m[:, :d]) * _dot(ya_ref[...], wba_ref[...]) \
        + jax.nn.sigmoid(m[:, d:]) * _dot(yh_ref[...], wbh_ref[...])
    x_new = x_ref[...] + _dot(merged.astype(BF16), wo_ref[...])
    ms = jnp.mean(x_new * x_new, axis=-1, keepdims=True)
    normed = x_new * lax.rsqrt(ms + NORM_EPS) * nw_ref[...]
    if last:
        out_refs[0][...] = normed
    else:
        out_refs[0][...] = x_new
        out_refs[1][...] = normed.astype(BF16)


def _merge(x2, h2, ya2, yh2, wm, wba, wbh, wo, nw, last):
    n, d = x2.shape
    tm = MERGE_TILE
    tile = lambda i: (i, 0)
    whole = lambda i: (0, 0)
    if last:
        out_shape = (jax.ShapeDtypeStruct((n, d), F32),)
        out_specs = (pl.BlockSpec((tm, d), tile),)
    else:
        out_shape = (jax.ShapeDtypeStruct((n, d), F32), jax.ShapeDtypeStruct((n, d), BF16))
        out_specs = (pl.BlockSpec((tm, d), tile), pl.BlockSpec((tm, d), tile))
    return pl.pallas_call(
        functools.partial(_merge_kernel, last),
        grid=(n // tm,),
        in_specs=[
            pl.BlockSpec((tm, d), tile), pl.BlockSpec((tm, d), tile),
            pl.BlockSpec((tm, d), tile), pl.BlockSpec((tm, d), tile),
            pl.BlockSpec((d, 2 * d), whole), pl.BlockSpec((d, d), whole),
            pl.BlockSpec((d, d), whole), pl.BlockSpec((d, d), whole),
            pl.BlockSpec((1, d), whole),
        ],
        out_specs=out_specs,
        out_shape=out_shape,
        compiler_params=pltpu.CompilerParams(
            dimension_semantics=("arbitrary",), vmem_limit_bytes=VMEM_LIMIT),
    )(x2, h2, ya2, yh2, wm, wba, wbh, wo, nw.reshape(1, d))


def _rope_tables(seq):
    rows = seq // GRID_W
    row = jnp.repeat(jnp.arange(rows), GRID_W).astype(F32)
    col = jnp.tile(jnp.arange(GRID_W), rows).astype(F32)
    inv_freq = ROPE_THETA ** (-jnp.arange(0, ROPE_AXIS_DIM, 2, dtype=F32) / ROPE_AXIS_DIM)
    ang_r = row[:, None] * inv_freq
    ang_c = col[:, None] * inv_freq
    cr, sr, cc, sc = jnp.cos(ang_r), jnp.sin(ang_r), jnp.cos(ang_c), jnp.sin(ang_c)
    cos = jnp.concatenate([cr, cr, cc, cc], axis=1)
    sin = jnp.concatenate([-sr, sr, -sc, sc], axis=1)
    return jnp.tile(cos, (1, 2)), jnp.tile(sin, (1, 2))


def _layer_weights(w_in_l, d_model):
    aw = ATTN_HEADS * ATTN_HEAD_DIM
    kvw = ATTN_KV_HEADS * ATTN_HEAD_DIM
    hw = HGRN_HEADS * HGRN_DK
    sizes = (aw, kvw, kvw, aw, hw, hw, hw, hw, hw, d_model, d_model)
    offs = [0]
    for sz in sizes:
        offs.append(offs[-1] + sz)
    parts = [w_in_l[:, offs[i]:offs[i + 1]].astype(BF16) for i in range(len(sizes))]
    wq, wk, wv, wag, whq, whf, whb, whi, whg, wma, wmh = parts
    g = ATTN_KV_HEADS
    w_a = jnp.concatenate([
        wq.reshape(d_model, g, GROUP_Q), wk.reshape(d_model, g, ATTN_HEAD_DIM),
        wv.reshape(d_model, g, ATTN_HEAD_DIM), wag.reshape(d_model, g, GROUP_Q)], axis=2)
    w_a = w_a.transpose(1, 0, 2)
    w_h = jnp.concatenate([w.reshape(d_model, HGRN_HEADS, HGRN_DK) for w in (whq, whf, whb, whi, whg)], axis=2)
    w_h = w_h.transpose(1, 0, 2)
    w_m = jnp.concatenate([wma, wmh], axis=1)
    return w_a, w_h, w_m


def kernel(x, w_in, norm_w, q_norm_w, k_norm_w, hgrn_lower_bounds, hgrn_norm_w, w_branch_attn, w_branch_hgrn,
           w_out, final_norm_w):
    b, s, d = x.shape
    depth = w_in.shape[0]
    assert s % Q_TILE == 0 and s % HGRN_CHUNK == 0 and s % GRID_W == 0 and (b * s) % MERGE_TILE == 0
    cos, sin = _rope_tables(s)
    x2 = x.reshape(b * s, d)
    h2 = _prenorm(x2, norm_w[0])
    out = None
    for layer in range(depth):
        w_a, w_h, w_m = _layer_weights(w_in[layer], d)
        h3 = h2.reshape(b, s, d)
        qw = jnp.tile(q_norm_w[layer], ATTN_GROUP).reshape(1, GROUP_Q)
        kw = jnp.concatenate([k_norm_w[layer], jnp.ones((ATTN_HEAD_DIM,), F32)]).reshape(1, GROUP_KV)
        ya = _attention(h3, w_a, cos, sin, qw, kw)
        yh = _hgrn(h3, w_h, hgrn_lower_bounds.astype(F32), hgrn_norm_w[layer].reshape(1, HGRN_DV), layer)
        last = layer == depth - 1
        nw = final_norm_w if last else norm_w[layer + 1]
        res = _merge(x2, h2, ya.reshape(b * s, -1), yh.reshape(b * s, -1), w_m,
                     w_branch_attn[layer].astype(BF16), w_branch_hgrn[layer].astype(BF16),
                     w_out[layer].astype(BF16), nw, last)
        if last:
            out = res[0]
        else:
            x2, h2 = res
    return out.reshape(b, s, d)
```

```python
import functools
import math

import jax
import jax.numpy as jnp
from jax import lax
from jax.experimental import pallas as pl
from jax.experimental.pallas import tpu as pltpu

F32 = jnp.float32
BF16 = jnp.bfloat16

GRID_W = 64
ATTN_HEADS = 16
ATTN_KV_HEADS = 4
ATTN_HEAD_DIM = 64
ATTN_GROUP = ATTN_HEADS // ATTN_KV_HEADS
ROPE_AXIS_DIM = ATTN_HEAD_DIM // 2
ROPE_THETA = 10000.0
HGRN_HEADS = 8
HGRN_DK = 128
HGRN_DV = 128
NORM_EPS = 1e-6

GROUP_Q = ATTN_GROUP * ATTN_HEAD_DIM
GROUP_KV = 2 * ATTN_HEAD_DIM
GROUP_COLS = GROUP_Q + GROUP_KV + GROUP_Q
HEAD_COLS = 5 * HGRN_DK

Q_TILE = 256
VT_ROWS = ATTN_HEAD_DIM + 16
KEY_BLOCK = 256
HGRN_CHUNK = 64
HGRN_MID = HGRN_CHUNK // 2 - 1
HGRN_SUPER = 256
PROJ_BLOCKS = 4
HGRN_SAFE_EXP = 64.0
MERGE_TILE = 512

VMEM_LIMIT = 56 * 1024 * 1024


def _dot(a, b):
    return jnp.dot(a, b, preferred_element_type=F32)


def _dot_nt(a, b):
    return lax.dot_general(a, b, (((1,), (1,)), ((), ())), preferred_element_type=F32)


def _dot_tn(a, b):
    return lax.dot_general(a, b, (((0,), (0,)), ((), ())), preferred_element_type=F32)


def _silu(x):
    return x * jax.nn.sigmoid(x)


def _split3(x):
    hi = x.astype(BF16)
    r1 = x - hi.astype(F32)
    mid = r1.astype(BF16)
    lo = (r1 - mid.astype(F32)).astype(BF16)
    return hi, mid, lo


def _split2(x):
    hi = x.astype(BF16)
    lo = (x - hi.astype(F32)).astype(BF16)
    return hi, lo


def _prenorm_kernel(x_ref, w_ref, h_ref):
    x = x_ref[...]
    ms = jnp.mean(x * x, axis=-1, keepdims=True)
    h_ref[...] = (x * lax.rsqrt(ms + NORM_EPS) * w_ref[...]).astype(BF16)


def _prenorm(x2, w):
    n, d = x2.shape
    tm = MERGE_TILE
    return pl.pallas_call(
        _prenorm_kernel,
        grid=(n // tm,),
        in_specs=[pl.BlockSpec((tm, d), lambda i: (i, 0)), pl.BlockSpec((1, d), lambda i: (0, 0))],
        out_specs=pl.BlockSpec((tm, d), lambda i: (i, 0)),
        out_shape=jax.ShapeDtypeStruct((n, d), BF16),
        compiler_params=pltpu.CompilerParams(dimension_semantics=("arbitrary",)),
    )(x2, w.reshape(1, d))


def _rope_partner(x):
    width = x.shape[-1]
    lane = lax.broadcasted_iota(jnp.int32, x.shape, 1)
    first = (lane & (ROPE_AXIS_DIM - 1)) < (ROPE_AXIS_DIM // 2)
    return jnp.where(first, pltpu.roll(x, width - ROPE_AXIS_DIM // 2, 1), pltpu.roll(x, ROPE_AXIS_DIM // 2, 1))


def _attn_kernel(h_ref, w_ref, cos_ref, sin_ref, cos_t_ref, sin_t_ref, qw_ref, kw_ref, out_ref, qt_ref, k_ref, vt_ref, gate_ref, s_ref, m_ref):
    seq = h_ref.shape[1]
    n_tiles = seq // Q_TILE
    hd = ATTN_HEAD_DIM

    w = w_ref[0]
    q_scale = qw_ref[...] * (hd ** -0.5 * math.log2(math.e))
    r = lax.broadcasted_iota(jnp.int32, (GROUP_Q, GROUP_Q), 0)
    c = lax.broadcasted_iota(jnp.int32, (GROUP_Q, GROUP_Q), 1)
    same_head = jnp.where((r ^ c) < hd, 1.0, 0.0).astype(BF16)
    rk = lax.broadcasted_iota(jnp.int32, (GROUP_KV, GROUP_KV), 0)
    ck = lax.broadcasted_iota(jnp.int32, (GROUP_KV, GROUP_KV), 1)
    k_block = jnp.where((rk < hd) & (ck < hd), 1.0, 0.0).astype(BF16)
    rb = seq // PROJ_BLOCKS
    vt_ref[hd:, :] = jnp.ones((VT_ROWS - hd, seq), BF16)

    for blk in range(PROJ_BLOCKS):
        rows = slice(blk * rb, (blk + 1) * rb)
        proj = _dot(h_ref[0, rows, :], w)
        q = proj[:, :GROUP_Q]
        kv = proj[:, GROUP_Q:GROUP_Q + GROUP_KV]
        gate_ref[rows, :] = _silu(proj[:, GROUP_Q + GROUP_KV:])

        hi, lo = _split2(q * q)
        ssq = _dot(hi, same_head) + _dot(lo, same_head)
        qn_t = (q * lax.rsqrt(ssq * (1.0 / hd) + NORM_EPS) * q_scale).T
        half = ROPE_AXIS_DIM // 2
        parts = []
        for grp in range(GROUP_Q // ROPE_AXIS_DIM):
            base = grp * ROPE_AXIS_DIM
            parts += [qn_t[base + half:base + 2 * half], qn_t[base:base + half]]
        partner_t = jnp.concatenate(parts, axis=0)
        cos_t = jnp.concatenate([cos_t_ref[:, rows]] * ATTN_GROUP, axis=0)
        sin_t = jnp.concatenate([sin_t_ref[:, rows]] * ATTN_GROUP, axis=0)
        qr_t = (qn_t * cos_t + partner_t * sin_t).astype(BF16)
        for t in range(rb // Q_TILE):
            for j in range(ATTN_GROUP):
                qt_ref[blk * (rb // Q_TILE) + t, :, j * Q_TILE:(j + 1) * Q_TILE] = \
                    qr_t[j * hd:(j + 1) * hd, t * Q_TILE:(t + 1) * Q_TILE]

        cos = cos_ref[rows, :]
        sin = sin_ref[rows, :]
        hi, lo = _split2(kv * kv)
        ssk = _dot(hi, k_block) + _dot(lo, k_block)
        kn = kv * lax.rsqrt(ssk * (1.0 / hd) + NORM_EPS) * kw_ref[...]
        kr = kn * cos + _rope_partner(kn) * sin
        k_ref[rows, :] = kr[:, :hd].astype(BF16)
        vt_ref[:hd, rows] = kv.T[hd:, :].astype(BF16)


    def step(nxt, cur):
        if cur is not None:
            m = m_ref[cur[1]]
        acc = None
        m_new = None
        for kb in range(seq // KEY_BLOCK):
            rows = slice(kb * KEY_BLOCK, (kb + 1) * KEY_BLOCK)
            if nxt is not None:
                sc = _dot(k_ref[rows, :], qt_ref[nxt[0]])
                s_ref[nxt[1], rows, :] = sc
                cm = jnp.max(sc, axis=0, keepdims=True)
                m_new = cm if m_new is None else jnp.maximum(m_new, cm)
            if cur is not None:
                p = jnp.exp2(s_ref[cur[1], rows, :] - m).astype(BF16)
                part = _dot(vt_ref[:, rows], p)
                acc = part if acc is None else acc + part
        if nxt is not None:
            m_ref[nxt[1]] = m_new
        if cur is not None:
            o = acc[:hd] / acc[hd:hd + 1]
            o = jnp.concatenate([o[:, j * Q_TILE:(j + 1) * Q_TILE] for j in range(ATTN_GROUP)], axis=0).T
            col = pl.multiple_of(cur[0] * Q_TILE, Q_TILE)
            out_ref[0, pl.ds(col, Q_TILE), :] = (o * gate_ref[pl.ds(col, Q_TILE), :]).astype(BF16)

    step((0, 0), None)

    def pair(j, carry):
        i = 2 * j
        step((i + 1, 1), (i, 0))
        step((i + 2, 0), (i + 1, 1))
        return carry

    lax.fori_loop(0, n_tiles // 2 - 1, pair, 0)
    step((n_tiles - 1, 1), (n_tiles - 2, 0))
    step(None, (n_tiles - 1, 1))


def _attention(h, w_a, cos, sin, cos_t, sin_t, qw, kw):
    b, s, d = h.shape
    g = w_a.shape[0]
    return pl.pallas_call(
        _attn_kernel,
        grid=(b, g),
        in_specs=[
            pl.BlockSpec((1, s, d), lambda i, j: (i, 0, 0)),
            pl.BlockSpec((1, d, GROUP_COLS), lambda i, j: (j, 0, 0)),
            pl.BlockSpec((s, GROUP_KV), lambda i, j: (0, 0)),
            pl.BlockSpec((s, GROUP_KV), lambda i, j: (0, 0)),
            pl.BlockSpec((ATTN_HEAD_DIM, s), lambda i, j: (0, 0)),
            pl.BlockSpec((ATTN_HEAD_DIM, s), lambda i, j: (0, 0)),
            pl.BlockSpec((1, GROUP_Q), lambda i, j: (0, 0)),
            pl.BlockSpec((1, GROUP_KV), lambda i, j: (0, 0)),
        ],
        out_specs=pl.BlockSpec((1, s, GROUP_Q), lambda i, j: (i, 0, j)),
        out_shape=jax.ShapeDtypeStruct((b, s, g * GROUP_Q), BF16),
        scratch_shapes=[
            pltpu.VMEM((s // Q_TILE, ATTN_HEAD_DIM, ATTN_GROUP * Q_TILE), BF16),
            pltpu.VMEM((s, ATTN_HEAD_DIM), BF16),
            pltpu.VMEM((VT_ROWS, s), BF16),
            pltpu.VMEM((s, GROUP_Q), F32),
            pltpu.VMEM((2, s, ATTN_GROUP * Q_TILE), F32),
            pltpu.VMEM((2, 1, ATTN_GROUP * Q_TILE), F32),
        ],
        compiler_params=pltpu.CompilerParams(
            dimension_semantics=("arbitrary", "arbitrary"), vmem_limit_bytes=VMEM_LIMIT),
    )(h, w_a, cos, sin, cos_t, sin_t, qw, kw)


def _forget_gate(z, lb):
    u = jnp.exp(-jnp.abs(z))
    t = 1.0 + u
    log_sig = jnp.minimum(z, 0.0) - jnp.log(t)
    sig_neg = jnp.where(z >= 0.0, u, 1.0) / t
    a = jnp.log(lb)
    c = jnp.log(1.0 - lb) + log_sig
    mx = jnp.maximum(a, c)
    g = mx + jnp.log(1.0 + jnp.exp(jnp.minimum(a, c) - mx))
    return g, (1.0 - lb) * sig_neg


def _hgrn_kernel(layer, h_ref, w_ref, lbp_ref, gw_ref, out_ref, q_ref, v_ref, k_ref, b_ref, oi_ref):
    seq = h_ref.shape[1]
    ch = HGRN_CHUNK
    sup = HGRN_SUPER
    n_chunks = seq // ch
    n_sup = seq // sup
    dk = HGRN_DK

    w = w_ref[0]
    rb = seq // PROJ_BLOCKS
    proj = jnp.concatenate([_dot(h_ref[0, i * rb:(i + 1) * rb, :], w) for i in range(PROJ_BLOCKS)], axis=0)
    q = _silu(proj[:, :dk]) * (dk ** -0.5)
    v = proj[:, 3 * dk:4 * dk]
    gate = _silu(proj[:, 4 * dk:])
    q_ref[...] = q
    v_ref[...] = v
    vb = v.astype(BF16)
    q3 = q.reshape(n_chunks, ch, dk)

    row = lax.broadcasted_iota(jnp.int32, (sup, sup), 0)
    colm = lax.broadcasted_iota(jnp.int32, (sup, sup), 1)
    same_chunk = (row ^ colm) < ch
    masks = (same_chunk & (colm <= row), same_chunk & (colm >= row))

    qe, ke, qh, kd, decay, viol = [], [], [], [], [], []
    for d in range(2):
        backward = d == 1
        z = proj[:, (1 + d) * dk:(2 + d) * dk]
        p = lbp_ref[d]
        depth = p.shape[0]
        pm = p[0:1]
        for i in range(1, depth):
            pm = jnp.maximum(pm, p[i:i + 1])
        ex = [jnp.exp(p[i:i + 1] - pm) for i in range(depth)]
        tot = ex[0]
        for i in range(1, depth):
            tot = tot + ex[i]
        first = ex[0] / tot
        cum = first
        for i in range(1, layer + 1):
            cum = cum + ex[i] / tot
        g, k = _forget_gate(z, cum - first)
        k_ref[d] = k

        tri_b = jnp.where(masks[d], 1.0, 0.0).astype(BF16)
        g2 = jnp.concatenate(_split2(g), axis=1)
        bs = []
        for i in range(n_sup):
            cs = _dot(tri_b, g2[i * sup:(i + 1) * sup])
            bs.append(cs[:, :dk] + cs[:, dk:])
        b = jnp.concatenate(bs, axis=0)
        b_ref[d] = b

        edge = 0 if backward else ch - 1
        mid = ch - 1 - HGRN_MID if backward else HGRN_MID
        b3 = b.reshape(n_chunks, ch, dk)
        k3 = k.reshape(n_chunks, ch, dk)
        b_edge = b3[:, edge:edge + 1, :]
        rel = b3 - b3[:, mid:mid + 1, :]
        viol.append(jnp.max(jnp.abs(rel)))
        qe.append((q3 * jnp.exp(rel)).astype(BF16).reshape(seq, dk))
        ke.append((k3 * jnp.exp(-rel)).astype(BF16).reshape(seq, dk))
        qh.append((q3 * jnp.exp(b3)).astype(BF16).reshape(seq, dk))
        kd.append((k3 * jnp.exp(b_edge - b3)).astype(BF16).reshape(seq, dk))
        decay.append(jnp.exp(b_edge))

    ois = []
    for i in range(n_sup):
        sl = slice(i * sup, (i + 1) * sup)
        a = jnp.where(masks[0], _dot_nt(qe[0][sl], ke[0][sl]), 0.0) \
            + jnp.where(masks[1], _dot_nt(qe[1][sl], ke[1][sl]), 0.0)
        ois.append(_dot(a.astype(BF16), vb[sl]))
    oi_ref[...] = jnp.concatenate(ois, axis=0)

    kd2 = jnp.concatenate(kd, axis=1)
    qh2 = jnp.concatenate(qh, axis=1)
    gs = [_dot_tn(vb[c * ch:(c + 1) * ch], kd2[c * ch:(c + 1) * ch]) for c in range(n_chunks)]
    entering = [[None] * n_chunks, [None] * n_chunks]
    for d in range(2):
        state = jnp.zeros((HGRN_DV, dk), F32)
        order = range(n_chunks - 1, -1, -1) if d == 1 else range(n_chunks)
        for c in order:
            entering[d][c] = state
            state = state * decay[d][c] + gs[c][:, d * dk:(d + 1) * dk]
    inter = []
    for c in range(n_chunks):
        st = jnp.concatenate([entering[0][c], entering[1][c]], axis=1).astype(BF16)
        inter.append(_dot_nt(qh2[c * ch:(c + 1) * ch], st))
    o_inter = jnp.concatenate(inter, axis=0)

    @pl.when(jnp.maximum(viol[0], viol[1]) > HGRN_SAFE_EXP)
    def _():
        def one_row(t, carry):
            t0 = pl.multiple_of((t // ch) * ch, ch)
            s_idx = t0 + lax.broadcasted_iota(jnp.int32, (ch, 1), 0)
            acc = jnp.zeros((1, HGRN_DV), F32)
            for d in range(2):
                bc = b_ref[d, pl.ds(t0, ch), :]
                bt = b_ref[d, pl.ds(t, 1), :]
                valid = (s_idx >= t) if d == 1 else (s_idx <= t)
                dec = jnp.where(valid, jnp.exp(jnp.minimum(bt - bc, 0.0)), 0.0)
                w = dec * (q_ref[pl.ds(t, 1), :] * k_ref[d, pl.ds(t0, ch), :])
                sc = jnp.sum(w, axis=1, keepdims=True)
                acc = acc + jnp.sum(sc * v_ref[pl.ds(t0, ch), :], axis=0, keepdims=True)
            oi_ref[pl.ds(t, 1), :] = acc
            return carry

        lax.fori_loop(0, seq, one_row, 0)

    o = o_inter + oi_ref[...]
    ms = jnp.mean(o * o, axis=-1, keepdims=True)
    out_ref[0] = (o * lax.rsqrt(ms + NORM_EPS) * gw_ref[...] * gate).astype(BF16)


def _hgrn(h, w_h, lb_params, gw, layer):
    b, s, d = h.shape
    nh = w_h.shape[0]
    ndir, depth, _ = lb_params.shape
    return pl.pallas_call(
        functools.partial(_hgrn_kernel, layer),
        grid=(b, nh),
        in_specs=[
            pl.BlockSpec((1, s, d), lambda i, j: (i, 0, 0)),
            pl.BlockSpec((1, d, HEAD_COLS), lambda i, j: (j, 0, 0)),
            pl.BlockSpec((ndir, depth, HGRN_DK), lambda i, j: (0, 0, j)),
            pl.BlockSpec((1, HGRN_DV), lambda i, j: (0, 0)),
        ],
        out_specs=pl.BlockSpec((1, s, HGRN_DV), lambda i, j: (i, 0, j)),
        out_shape=jax.ShapeDtypeStruct((b, s, nh * HGRN_DV), BF16),
        scratch_shapes=[
            pltpu.VMEM((s, HGRN_DK), F32),
            pltpu.VMEM((s, HGRN_DV), F32),
            pltpu.VMEM((2, s, HGRN_DK), F32),
            pltpu.VMEM((2, s, HGRN_DK), F32),
            pltpu.VMEM((s, HGRN_DV), F32),
        ],
        compiler_params=pltpu.CompilerParams(
            dimension_semantics=("arbitrary", "arbitrary"), vmem_limit_bytes=VMEM_LIMIT),
    )(h, w_h, lb_params, gw)


def _merge_kernel(last, x_ref, h_ref, ya_ref, yh_ref, wm_ref, wba_ref, wbh_ref, wo_ref, nw_ref, *out_refs):
    d = x_ref.shape[1]
    m = _dot(h_ref[...], wm_ref[...])
    merged = jax.nn.sigmoid(m[:, :d]) * _dot(ya_ref[...], wba_ref[...]) \
        + jax.nn.sigmoid(m[:, d:]) * _dot(yh_ref[...], wbh_ref[...])
    x_new = x_ref[...] + _dot(merged.astype(BF16), wo_ref[...])
    ms = jnp.mean(x_new * x_new, axis=-1, keepdims=True)
    normed = x_new * lax.rsqrt(ms + NORM_EPS) * nw_ref[...]
    if last:
        out_refs[0][...] = normed
    else:
        out_refs[0][...] = x_new
        out_refs[1][...] = normed.astype(BF16)


def _merge(x2, h2, ya2, yh2, wm, wba, wbh, wo, nw, last):
    n, d = x2.shape
    tm = MERGE_TILE
    tile = lambda i: (i, 0)
    whole = lambda i: (0, 0)
    if last:
        out_shape = (jax.ShapeDtypeStruct((n, d), F32),)
        out_specs = (pl.BlockSpec((tm, d), tile),)
    else:
        out_shape = (jax.ShapeDtypeStruct((n, d), F32), jax.ShapeDtypeStruct((n, d), BF16))
        out_specs = (pl.BlockSpec((tm, d), tile), pl.BlockSpec((tm, d), tile))
    return pl.pallas_call(
        functools.partial(_merge_kernel, last),
        grid=(n // tm,),
        in_specs=[
            pl.BlockSpec((tm, d), tile), pl.BlockSpec((tm, d), tile),
            pl.BlockSpec((tm, d), tile), pl.BlockSpec((tm, d), tile),
            pl.BlockSpec((d, 2 * d), whole), pl.BlockSpec((d, d), whole),
            pl.BlockSpec((d, d), whole), pl.BlockSpec((d, d), whole),
            pl.BlockSpec((1, d), whole),
        ],
        out_specs=out_specs,
        out_shape=out_shape,
        compiler_params=pltpu.CompilerParams(
            dimension_semantics=("arbitrary",), vmem_limit_bytes=VMEM_LIMIT),
    )(x2, h2, ya2, yh2, wm, wba, wbh, wo, nw.reshape(1, d))


def _rope_tables(seq):
    rows = seq // GRID_W
    row = jnp.repeat(jnp.arange(rows), GRID_W).astype(F32)
    col = jnp.tile(jnp.arange(GRID_W), rows).astype(F32)
    inv_freq = ROPE_THETA ** (-jnp.arange(0, ROPE_AXIS_DIM, 2, dtype=F32) / ROPE_AXIS_DIM)
    ang_r = row[:, None] * inv_freq
    ang_c = col[:, None] * inv_freq
    cr, sr, cc, sc = jnp.cos(ang_r), jnp.sin(ang_r), jnp.cos(ang_c), jnp.sin(ang_c)
    cos = jnp.concatenate([cr, cr, cc, cc], axis=1)
    sin = jnp.concatenate([-sr, sr, -sc, sc], axis=1)
    return jnp.tile(cos, (1, 2)), jnp.tile(sin, (1, 2)), cos.T, sin.T


def _layer_weights(w_in_l, d_model):
    aw = ATTN_HEADS * ATTN_HEAD_DIM
    kvw = ATTN_KV_HEADS * ATTN_HEAD_DIM
    hw = HGRN_HEADS * HGRN_DK
    sizes = (aw, kvw, kvw, aw, hw, hw, hw, hw, hw, d_model, d_model)
    offs = [0]
    for sz in sizes:
        offs.append(offs[-1] + sz)
    parts = [w_in_l[:, offs[i]:offs[i + 1]].astype(BF16) for i in range(len(sizes))]
    wq, wk, wv, wag, whq, whf, whb, whi, whg, wma, wmh = parts
    g = ATTN_KV_HEADS
    w_a = jnp.concatenate([
        wq.reshape(d_model, g, GROUP_Q), wk.reshape(d_model, g, ATTN_HEAD_DIM),
        wv.reshape(d_model, g, ATTN_HEAD_DIM), wag.reshape(d_model, g, GROUP_Q)], axis=2)
    w_a = w_a.transpose(1, 0, 2)
    w_h = jnp.concatenate([w.reshape(d_model, HGRN_HEADS, HGRN_DK) for w in (whq, whf, whb, whi, whg)], axis=2)
    w_h = w_h.transpose(1, 0, 2)
    w_m = jnp.concatenate([wma, wmh], axis=1)
    return w_a, w_h, w_m


def kernel(x, w_in, norm_w, q_norm_w, k_norm_w, hgrn_lower_bounds, hgrn_norm_w, w_branch_attn, w_branch_hgrn,
           w_out, final_norm_w):
    b, s, d = x.shape
    depth = w_in.shape[0]
    assert s % Q_TILE == 0 and s % HGRN_CHUNK == 0 and s % GRID_W == 0 and (b * s) % MERGE_TILE == 0
    cos, sin, cos_t, sin_t = _rope_tables(s)
    x2 = x.reshape(b * s, d)
    h2 = _prenorm(x2, norm_w[0])
    out = None
    for layer in range(depth):
        w_a, w_h, w_m = _layer_weights(w_in[layer], d)
        h3 = h2.reshape(b, s, d)
        qw = jnp.tile(q_norm_w[layer], ATTN_GROUP).reshape(1, GROUP_Q)
        kw = jnp.concatenate([k_norm_w[layer], jnp.ones((ATTN_HEAD_DIM,), F32)]).reshape(1, GROUP_KV)
        ya = _attention(h3, w_a, cos, sin, cos_t, sin_t, qw, kw)
        yh = _hgrn(h3, w_h, hgrn_lower_bounds.astype(F32), hgrn_norm_w[layer].reshape(1, HGRN_DV), layer)
        last = layer == depth - 1
        nw = final_norm_w if last else norm_w[layer + 1]
        res = _merge(x2, h2, ya.reshape(b * s, -1), yh.reshape(b * s, -1), w_m,
                     w_branch_attn[layer].astype(BF16), w_branch_hgrn[layer].astype(BF16),
                     w_out[layer].astype(BF16), nw, last)
        if last:
            out = res[0]
        else:
            x2, h2 = res
    return out.reshape(b, s, d)
```

```python
import functools
import math

import jax
import jax.numpy as jnp
from jax import lax
from jax.experimental import pallas as pl
from jax.experimental.pallas import tpu as pltpu

F32 = jnp.float32
BF16 = jnp.bfloat16

GRID_W = 64
ATTN_HEADS = 16
ATTN_KV_HEADS = 4
ATTN_HEAD_DIM = 64
ATTN_GROUP = ATTN_HEADS // ATTN_KV_HEADS
ROPE_AXIS_DIM = ATTN_HEAD_DIM // 2
ROPE_THETA = 10000.0
HGRN_HEADS = 8
HGRN_DK = 128
HGRN_DV = 128
NORM_EPS = 1e-6

GROUP_Q = ATTN_GROUP * ATTN_HEAD_DIM
GROUP_KV = 2 * ATTN_HEAD_DIM
GROUP_COLS = GROUP_Q + GROUP_KV + GROUP_Q
HEAD_COLS = 5 * HGRN_DK

Q_TILE = 256
VT_ROWS = ATTN_HEAD_DIM + 16
KEY_BLOCK = 256
HGRN_CHUNK = 64
HGRN_MID = HGRN_CHUNK // 2 - 1
HGRN_SUPER = 256
PROJ_BLOCKS = 4
HGRN_HEADS_PER_STEP = 2
HGRN_SAFE_EXP2 = 90.0
LOG2_E = math.log2(math.e)
MERGE_TILE = 512

VMEM_LIMIT = 56 * 1024 * 1024


def _dot(a, b):
    return jnp.dot(a, b, preferred_element_type=F32)


def _dot_nt(a, b):
    return lax.dot_general(a, b, (((1,), (1,)), ((), ())), preferred_element_type=F32)


def _dot_tn(a, b):
    return lax.dot_general(a, b, (((0,), (0,)), ((), ())), preferred_element_type=F32)


def _silu(x):
    return x * jax.nn.sigmoid(x)


def _split3(x):
    hi = x.astype(BF16)
    r1 = x - hi.astype(F32)
    mid = r1.astype(BF16)
    lo = (r1 - mid.astype(F32)).astype(BF16)
    return hi, mid, lo


def _split2(x):
    hi = x.astype(BF16)
    lo = (x - hi.astype(F32)).astype(BF16)
    return hi, lo


def _prenorm_kernel(x_ref, w_ref, h_ref):
    x = x_ref[...]
    ms = jnp.mean(x * x, axis=-1, keepdims=True)
    h_ref[...] = (x * lax.rsqrt(ms + NORM_EPS) * w_ref[...]).astype(BF16)


def _prenorm(x2, w):
    n, d = x2.shape
    tm = MERGE_TILE
    return pl.pallas_call(
        _prenorm_kernel,
        grid=(n // tm,),
        in_specs=[pl.BlockSpec((tm, d), lambda i: (i, 0)), pl.BlockSpec((1, d), lambda i: (0, 0))],
        out_specs=pl.BlockSpec((tm, d), lambda i: (i, 0)),
        out_shape=jax.ShapeDtypeStruct((n, d), BF16),
        compiler_params=pltpu.CompilerParams(dimension_semantics=("arbitrary",)),
    )(x2, w.reshape(1, d))


def _rope_partner(x):
    width = x.shape[-1]
    lane = lax.broadcasted_iota(jnp.int32, x.shape, 1)
    first = (lane & (ROPE_AXIS_DIM - 1)) < (ROPE_AXIS_DIM // 2)
    return jnp.where(first, pltpu.roll(x, width - ROPE_AXIS_DIM // 2, 1), pltpu.roll(x, ROPE_AXIS_DIM // 2, 1))


def _attn_kernel(h_ref, w_ref, cos_ref, sin_ref, cos_t_ref, sin_t_ref, qw_ref, kw_ref, out_ref, qt_ref, k_ref, vt_ref, gate_ref, s_ref, m_ref):
    seq = h_ref.shape[1]
    n_tiles = seq // Q_TILE
    hd = ATTN_HEAD_DIM

    w = w_ref[0]
    q_scale = qw_ref[...] * (hd ** -0.5 * LOG2_E)
    r = lax.broadcasted_iota(jnp.int32, (GROUP_Q, GROUP_Q), 0)
    c = lax.broadcasted_iota(jnp.int32, (GROUP_Q, GROUP_Q), 1)
    same_head = jnp.where((r ^ c) < hd, 1.0, 0.0).astype(BF16)
    rb = seq // PROJ_BLOCKS
    is_k = lax.broadcasted_iota(jnp.int32, (rb, GROUP_KV), 1) < hd
    vt_ref[hd:, :] = jnp.ones((VT_ROWS - hd, seq), BF16)

    for blk in range(PROJ_BLOCKS):
        rows = slice(blk * rb, (blk + 1) * rb)
        proj = _dot(h_ref[0, rows, :], w)
        q = proj[:, :GROUP_Q]
        kv = proj[:, GROUP_Q:GROUP_Q + GROUP_KV]
        gate_ref[rows, :] = _silu(proj[:, GROUP_Q + GROUP_KV:])

        hi, lo = _split2(q * q)
        ssq = _dot(hi, same_head) + _dot(lo, same_head)
        qn_t = (q * lax.rsqrt(ssq * (1.0 / hd) + NORM_EPS) * q_scale).T
        half = ROPE_AXIS_DIM // 2
        parts = []
        for grp in range(GROUP_Q // ROPE_AXIS_DIM):
            base = grp * ROPE_AXIS_DIM
            parts += [qn_t[base + half:base + 2 * half], qn_t[base:base + half]]
        partner_t = jnp.concatenate(parts, axis=0)
        cos_t = jnp.concatenate([cos_t_ref[:, rows]] * ATTN_GROUP, axis=0)
        sin_t = jnp.concatenate([sin_t_ref[:, rows]] * ATTN_GROUP, axis=0)
        qr_t = (qn_t * cos_t + partner_t * sin_t).astype(BF16)
        for t in range(rb // Q_TILE):
            for j in range(ATTN_GROUP):
                qt_ref[blk * (rb // Q_TILE) + t, :, j * Q_TILE:(j + 1) * Q_TILE] = \
                    qr_t[j * hd:(j + 1) * hd, t * Q_TILE:(t + 1) * Q_TILE]

        cos = cos_ref[rows, :]
        sin = sin_ref[rows, :]
        ssk = jnp.sum(jnp.where(is_k, kv * kv, 0.0), axis=-1, keepdims=True)
        kn = kv * lax.rsqrt(ssk * (1.0 / hd) + NORM_EPS) * kw_ref[...]
        kr = kn * cos + _rope_partner(kn) * sin
        k_ref[rows, :] = kr[:, :hd].astype(BF16)
        vt_ref[:hd, rows] = kv.T[hd:, :].astype(BF16)


    def step(nxt, cur):
        if cur is not None:
            m = m_ref[cur[1]]
        acc = None
        m_new = None
        for kb in range(seq // KEY_BLOCK):
            rows = slice(kb * KEY_BLOCK, (kb + 1) * KEY_BLOCK)
            if nxt is not None:
                sc = _dot(k_ref[rows, :], qt_ref[nxt[0]])
                s_ref[nxt[1], rows, :] = sc
                cm = jnp.max(sc, axis=0, keepdims=True)
                m_new = cm if m_new is None else jnp.maximum(m_new, cm)
            if cur is not None:
                p = jnp.exp2(s_ref[cur[1], rows, :] - m).astype(BF16)
                part = _dot(vt_ref[:, rows], p)
                acc = part if acc is None else acc + part
        if nxt is not None:
            m_ref[nxt[1]] = m_new
        if cur is not None:
            o = acc[:hd] / acc[hd:hd + 1]
            o = jnp.concatenate([o[:, j * Q_TILE:(j + 1) * Q_TILE] for j in range(ATTN_GROUP)], axis=0).T
            col = pl.multiple_of(cur[0] * Q_TILE, Q_TILE)
            out_ref[0, pl.ds(col, Q_TILE), :] = (o * gate_ref[pl.ds(col, Q_TILE), :]).astype(BF16)

    step((0, 0), None)

    def pair(j, carry):
        i = 2 * j
        step((i + 1, 1), (i, 0))
        step((i + 2, 0), (i + 1, 1))
        return carry

    lax.fori_loop(0, n_tiles // 2 - 1, pair, 0)
    step((n_tiles - 1, 1), (n_tiles - 2, 0))
    step(None, (n_tiles - 1, 1))


def _attention(h, w_a, cos, sin, cos_t, sin_t, qw, kw):
    b, s, d = h.shape
    g = w_a.shape[0]
    return pl.pallas_call(
        _attn_kernel,
        grid=(b, g),
        in_specs=[
            pl.BlockSpec((1, s, d), lambda i, j: (i, 0, 0)),
            pl.BlockSpec((1, d, GROUP_COLS), lambda i, j: (j, 0, 0)),
            pl.BlockSpec((s, GROUP_KV), lambda i, j: (0, 0)),
            pl.BlockSpec((s, GROUP_KV), lambda i, j: (0, 0)),
            pl.BlockSpec((ATTN_HEAD_DIM, s), lambda i, j: (0, 0)),
            pl.BlockSpec((ATTN_HEAD_DIM, s), lambda i, j: (0, 0)),
            pl.BlockSpec((1, GROUP_Q), lambda i, j: (0, 0)),
            pl.BlockSpec((1, GROUP_KV), lambda i, j: (0, 0)),
        ],
        out_specs=pl.BlockSpec((1, s, GROUP_Q), lambda i, j: (i, 0, j)),
        out_shape=jax.ShapeDtypeStruct((b, s, g * GROUP_Q), BF16),
        scratch_shapes=[
            pltpu.VMEM((s // Q_TILE, ATTN_HEAD_DIM, ATTN_GROUP * Q_TILE), BF16),
            pltpu.VMEM((s, ATTN_HEAD_DIM), BF16),
            pltpu.VMEM((VT_ROWS, s), BF16),
            pltpu.VMEM((s, GROUP_Q), F32),
            pltpu.VMEM((2, s, ATTN_GROUP * Q_TILE), F32),
            pltpu.VMEM((2, 1, ATTN_GROUP * Q_TILE), F32),
        ],
        compiler_params=pltpu.CompilerParams(
            dimension_semantics=("arbitrary", "arbitrary"), vmem_limit_bytes=VMEM_LIMIT),
    )(h, w_a, cos, sin, cos_t, sin_t, qw, kw)


def _forget_gate(z, lb):
    u = jnp.exp(-jnp.abs(z))
    t = 1.0 + u
    log_sig = jnp.minimum(z, 0.0) - jnp.log(t)
    sig_neg = jnp.where(z >= 0.0, u, 1.0) / t
    a = jnp.log(lb)
    c = jnp.log(1.0 - lb) + log_sig
    mx = jnp.maximum(a, c)
    g = mx + jnp.log(1.0 + jnp.exp(jnp.minimum(a, c) - mx))
    return g * LOG2_E, (1.0 - lb) * sig_neg


def _hgrn_kernel(layer, h_ref, w_ref, lbp_ref, gw_ref, out_ref, q_ref, v_ref, k_ref, b_ref, oi_ref, gate_ref):
    seq = h_ref.shape[1]
    ch = HGRN_CHUNK
    sup = HGRN_SUPER
    n_chunks = seq // ch
    dk = HGRN_DK
    rb = seq // PROJ_BLOCKS
    blk_chunks = rb // ch
    heads = range(HGRN_HEADS_PER_STEP)

    row = lax.broadcasted_iota(jnp.int32, (sup, sup), 0)
    colm = lax.broadcasted_iota(jnp.int32, (sup, sup), 1)
    same_chunk = (row ^ colm) < ch
    masks = (same_chunk & (colm <= row), same_chunk & (colm >= row))
    tri = [jnp.where(mk, 1.0, 0.0).astype(BF16) for mk in masks]

    lbs = []
    for d in range(2):
        p = lbp_ref[d]
        depth = p.shape[0]
        pm = p[0:1]
        for i in range(1, depth):
            pm = jnp.maximum(pm, p[i:i + 1])
        ex = [jnp.exp(p[i:i + 1] - pm) for i in range(depth)]
        tot = ex[0]
        for i in range(1, depth):
            tot = tot + ex[i]
        first = ex[0] / tot
        cum = first
        for i in range(1, layer + 1):
            cum = cum + ex[i] / tot
        lbs.append(cum - first)

    w = w_ref[0]
    decay = [[[], []] for _ in heads]
    gs = [[] for _ in heads]
    qh2 = [[] for _ in heads]
    viol = jnp.zeros((1, 1, dk), F32)
    for blk in range(PROJ_BLOCKS):
        rows = slice(blk * rb, (blk + 1) * rb)
        proj_all = _dot(h_ref[0, rows, :], w)
        for hh in heads:
            proj = proj_all[:, hh * HEAD_COLS:(hh + 1) * HEAD_COLS]
            q = _silu(proj[:, :dk]) * (dk ** -0.5)
            v = proj[:, 3 * dk:4 * dk]
            gate_ref[hh, rows, :] = _silu(proj[:, 4 * dk:])
            q_ref[hh, rows, :] = q
            v_ref[hh, rows, :] = v
            vb = v.astype(BF16)
            q3 = q.reshape(blk_chunks, ch, dk)

            qe, ke, qh, kd = [], [], [], []
            for d in range(2):
                backward = d == 1
                g, k = _forget_gate(proj[:, (1 + d) * dk:(2 + d) * dk],
                                    lbs[d][:, hh * dk:(hh + 1) * dk])
                k_ref[hh, d, rows, :] = k
                g2 = jnp.concatenate(_split2(g), axis=1)
                bs = []
                for i in range(rb // sup):
                    cs = _dot(tri[d], g2[i * sup:(i + 1) * sup])
                    bs.append(cs[:, :dk] + cs[:, dk:])
                b = jnp.concatenate(bs, axis=0)
                b_ref[hh, d, rows, :] = b

                edge = 0 if backward else ch - 1
                mid = ch - 1 - HGRN_MID if backward else HGRN_MID
                b3 = b.reshape(blk_chunks, ch, dk)
                k3 = k.reshape(blk_chunks, ch, dk)
                b_edge = b3[:, edge:edge + 1, :]
                b_mid = b3[:, mid:mid + 1, :]
                rel = b3 - b_mid
                ends = jnp.maximum(jnp.abs(b3[:, 0:1, :] - b_mid), jnp.abs(b3[:, ch - 1:ch, :] - b_mid))
                viol = jnp.maximum(viol, jnp.max(ends, axis=0, keepdims=True))
                qe.append((q3 * jnp.exp2(rel)).astype(BF16).reshape(rb, dk))
                ke.append((k3 * jnp.exp2(-rel)).astype(BF16).reshape(rb, dk))
                qh.append((q3 * jnp.exp2(b3)).astype(BF16).reshape(rb, dk))
                kd.append((k3 * jnp.exp2(b_edge - b3)).astype(BF16).reshape(rb, dk))
                e = jnp.exp2(b_edge)
                decay[hh][d] += [e[c] for c in range(blk_chunks)]

            for i in range(rb // sup):
                sl = slice(i * sup, (i + 1) * sup)
                a = jnp.where(masks[0], _dot_nt(qe[0][sl], ke[0][sl]), 0.0) \
                    + jnp.where(masks[1], _dot_nt(qe[1][sl], ke[1][sl]), 0.0)
                oi_ref[hh, blk * rb + i * sup:blk * rb + (i + 1) * sup, :] = _dot(a.astype(BF16), vb[sl])
            kd2 = jnp.concatenate(kd, axis=1)
            gs[hh] += [_dot_tn(vb[c * ch:(c + 1) * ch], kd2[c * ch:(c + 1) * ch]) for c in range(blk_chunks)]
            qh2[hh].append(jnp.concatenate(qh, axis=1))

    o_inter = []
    for hh in heads:
        entering = [[None] * n_chunks, [None] * n_chunks]
        for d in range(2):
            state = jnp.zeros((HGRN_DV, dk), F32)
            order = range(n_chunks - 1, -1, -1) if d == 1 else range(n_chunks)
            for c in order:
                entering[d][c] = state
                state = state * decay[hh][d][c] + gs[hh][c][:, d * dk:(d + 1) * dk]
        inter = []
        for c in range(n_chunks):
            st = jnp.concatenate([entering[0][c], entering[1][c]], axis=1).astype(BF16)
            lc = c % blk_chunks
            inter.append(_dot_nt(qh2[hh][c // blk_chunks][lc * ch:(lc + 1) * ch], st))
        o_inter.append(jnp.concatenate(inter, axis=0))

    @pl.when(jnp.max(viol) > HGRN_SAFE_EXP2)
    def _():
        def one_row(t, carry):
            t0 = pl.multiple_of((t // ch) * ch, ch)
            s_idx = t0 + lax.broadcasted_iota(jnp.int32, (ch, 1), 0)
            for hh in heads:
                acc = jnp.zeros((1, HGRN_DV), F32)
                for d in range(2):
                    bc = b_ref[hh, d, pl.ds(t0, ch), :]
                    bt = b_ref[hh, d, pl.ds(t, 1), :]
                    valid = (s_idx >= t) if d == 1 else (s_idx <= t)
                    dec = jnp.where(valid, jnp.exp2(jnp.minimum(bt - bc, 0.0)), 0.0)
                    wgt = dec * (q_ref[hh, pl.ds(t, 1), :] * k_ref[hh, d, pl.ds(t0, ch), :])
                    sc = jnp.sum(wgt, axis=1, keepdims=True)
                    acc = acc + jnp.sum(sc * v_ref[hh, pl.ds(t0, ch), :], axis=0, keepdims=True)
                oi_ref[hh, pl.ds(t, 1), :] = acc
            return carry

        lax.fori_loop(0, seq, one_row, 0)

    for hh in heads:
        o = o_inter[hh] + oi_ref[hh]
        ms = jnp.mean(o * o, axis=-1, keepdims=True)
        out_ref[0, :, hh * HGRN_DV:(hh + 1) * HGRN_DV] = \
            (o * lax.rsqrt(ms + NORM_EPS) * gw_ref[...] * gate_ref[hh]).astype(BF16)


def _hgrn(h, w_h, lb_params, gw, layer):
    b, s, d = h.shape
    hps = HGRN_HEADS_PER_STEP
    n_steps = w_h.shape[0]
    ndir, depth, _ = lb_params.shape
    return pl.pallas_call(
        functools.partial(_hgrn_kernel, layer),
        grid=(b, n_steps),
        in_specs=[
            pl.BlockSpec((1, s, d), lambda i, j: (i, 0, 0)),
            pl.BlockSpec((1, d, hps * HEAD_COLS), lambda i, j: (j, 0, 0)),
            pl.BlockSpec((ndir, depth, hps * HGRN_DK), lambda i, j: (0, 0, j)),
            pl.BlockSpec((1, HGRN_DV), lambda i, j: (0, 0)),
        ],
        out_specs=pl.BlockSpec((1, s, hps * HGRN_DV), lambda i, j: (i, 0, j)),
        out_shape=jax.ShapeDtypeStruct((b, s, n_steps * hps * HGRN_DV), BF16),
        scratch_shapes=[
            pltpu.VMEM((hps, s, HGRN_DK), F32),
            pltpu.VMEM((hps, s, HGRN_DV), F32),
            pltpu.VMEM((hps, 2, s, HGRN_DK), F32),
            pltpu.VMEM((hps, 2, s, HGRN_DK), F32),
            pltpu.VMEM((hps, s, HGRN_DV), F32),
            pltpu.VMEM((hps, s, HGRN_DV), F32),
        ],
        compiler_params=pltpu.CompilerParams(
            dimension_semantics=("arbitrary", "arbitrary"), vmem_limit_bytes=VMEM_LIMIT),
    )(h, w_h, lb_params, gw)


def _merge_kernel(last, x_ref, h_ref, ya_ref, yh_ref, wm_ref, wba_ref, wbh_ref, wo_ref, nw_ref, *out_refs):
    d = x_ref.shape[1]
    m = _dot(h_ref[...], wm_ref[...])
    merged = jax.nn.sigmoid(m[:, :d]) * _dot(ya_ref[...], wba_ref[...]) \
        + jax.nn.sigmoid(m[:, d:]) * _dot(yh_ref[...], wbh_ref[...])
    x_new = x_ref[...] + _dot(merged.astype(BF16), wo_ref[...])
    ms = jnp.mean(x_new * x_new, axis=-1, keepdims=True)
    normed = x_new * lax.rsqrt(ms + NORM_EPS) * nw_ref[...]
    if last:
        out_refs[0][...] = normed
    else:
        out_refs[0][...] = x_new
        out_refs[1][...] = normed.astype(BF16)


def _merge(x2, h2, ya2, yh2, wm, wba, wbh, wo, nw, last):
    n, d = x2.shape
    tm = MERGE_TILE
    tile = lambda i: (i, 0)
    whole = lambda i: (0, 0)
    if last:
        out_shape = (jax.ShapeDtypeStruct((n, d), F32),)
        out_specs = (pl.BlockSpec((tm, d), tile),)
    else:
        out_shape = (jax.ShapeDtypeStruct((n, d), F32), jax.ShapeDtypeStruct((n, d), BF16))
        out_specs = (pl.BlockSpec((tm, d), tile), pl.BlockSpec((tm, d), tile))
    return pl.pallas_call(
        functools.partial(_merge_kernel, last),
        grid=(n // tm,),
        in_specs=[
            pl.BlockSpec((tm, d), tile), pl.BlockSpec((tm, d), tile),
            pl.BlockSpec((tm, d), tile), pl.BlockSpec((tm, d), tile),
            pl.BlockSpec((d, 2 * d), whole), pl.BlockSpec((d, d), whole),
            pl.BlockSpec((d, d), whole), pl.BlockSpec((d, d), whole),
            pl.BlockSpec((1, d), whole),
        ],
        out_specs=out_specs,
        out_shape=out_shape,
        compiler_params=pltpu.CompilerParams(
            dimension_semantics=("arbitrary",), vmem_limit_bytes=VMEM_LIMIT),
    )(x2, h2, ya2, yh2, wm, wba, wbh, wo, nw.reshape(1, d))


def _rope_tables(seq):
    rows = seq // GRID_W
    row = jnp.repeat(jnp.arange(rows), GRID_W).astype(F32)
    col = jnp.tile(jnp.arange(GRID_W), rows).astype(F32)
    inv_freq = ROPE_THETA ** (-jnp.arange(0, ROPE_AXIS_DIM, 2, dtype=F32) / ROPE_AXIS_DIM)
    ang_r = row[:, None] * inv_freq
    ang_c = col[:, None] * inv_freq
    cr, sr, cc, sc = jnp.cos(ang_r), jnp.sin(ang_r), jnp.cos(ang_c), jnp.sin(ang_c)
    cos = jnp.concatenate([cr, cr, cc, cc], axis=1)
    sin = jnp.concatenate([-sr, sr, -sc, sc], axis=1)
    return jnp.tile(cos, (1, 2)), jnp.tile(sin, (1, 2)), cos.T, sin.T


def _layer_weights(w_in_l, d_model):
    aw = ATTN_HEADS * ATTN_HEAD_DIM
    kvw = ATTN_KV_HEADS * ATTN_HEAD_DIM
    hw = HGRN_HEADS * HGRN_DK
    sizes = (aw, kvw, kvw, aw, hw, hw, hw, hw, hw, d_model, d_model)
    offs = [0]
    for sz in sizes:
        offs.append(offs[-1] + sz)
    parts = [w_in_l[:, offs[i]:offs[i + 1]].astype(BF16) for i in range(len(sizes))]
    wq, wk, wv, wag, whq, whf, whb, whi, whg, wma, wmh = parts
    g = ATTN_KV_HEADS
    w_a = jnp.concatenate([
        wq.reshape(d_model, g, GROUP_Q), wk.reshape(d_model, g, ATTN_HEAD_DIM),
        wv.reshape(d_model, g, ATTN_HEAD_DIM), wag.reshape(d_model, g, GROUP_Q)], axis=2)
    w_a = w_a.transpose(1, 0, 2)
    w_h = jnp.concatenate([w.reshape(d_model, HGRN_HEADS, HGRN_DK) for w in (whq, whf, whb, whi, whg)], axis=2)
    w_h = w_h.reshape(d_model, HGRN_HEADS // HGRN_HEADS_PER_STEP, HGRN_HEADS_PER_STEP * HEAD_COLS)
    w_h = w_h.transpose(1, 0, 2)
    w_m = jnp.concatenate([wma, wmh], axis=1)
    return w_a, w_h, w_m


def kernel(x, w_in, norm_w, q_norm_w, k_norm_w, hgrn_lower_bounds, hgrn_norm_w, w_branch_attn, w_branch_hgrn,
           w_out, final_norm_w):
    b, s, d = x.shape
    depth = w_in.shape[0]
    assert s % Q_TILE == 0 and s % HGRN_CHUNK == 0 and s % GRID_W == 0 and (b * s) % MERGE_TILE == 0
    cos, sin, cos_t, sin_t = _rope_tables(s)
    x2 = x.reshape(b * s, d)
    h2 = _prenorm(x2, norm_w[0])
    out = None
    for layer in range(depth):
        w_a, w_h, w_m = _layer_weights(w_in[layer], d)
        h3 = h2.reshape(b, s, d)
        qw = jnp.tile(q_norm_w[layer], ATTN_GROUP).reshape(1, GROUP_Q)
        kw = jnp.concatenate([k_norm_w[layer], jnp.ones((ATTN_HEAD_DIM,), F32)]).reshape(1, GROUP_KV)
        ya = _attention(h3, w_a, cos, sin, cos_t, sin_t, qw, kw)
        yh = _hgrn(h3, w_h, hgrn_lower_bounds.astype(F32), hgrn_norm_w[layer].reshape(1, HGRN_DV), layer)
        last = layer == depth - 1
        nw = final_norm_w if last else norm_w[layer + 1]
        res = _merge(x2, h2, ya.reshape(b * s, -1), yh.reshape(b * s, -1), w_m,
                     w_branch_attn[layer].astype(BF16), w_branch_hgrn[layer].astype(BF16),
                     w_out[layer].astype(BF16), nw, last)
        if last:
            out = res[0]
        else:
            x2, h2 = res
    return out.reshape(b, s, d)
```

```python
import functools
import math

import jax
import jax.numpy as jnp
from jax import lax
from jax.experimental import pallas as pl
from jax.experimental.pallas import tpu as pltpu

F32 = jnp.float32
BF16 = jnp.bfloat16

GRID_W = 64
ATTN_HEADS = 16
ATTN_KV_HEADS = 4
ATTN_HEAD_DIM = 64
ATTN_GROUP = ATTN_HEADS // ATTN_KV_HEADS
ROPE_AXIS_DIM = ATTN_HEAD_DIM // 2
ROPE_THETA = 10000.0
HGRN_HEADS = 8
HGRN_DK = 128
HGRN_DV = 128
NORM_EPS = 1e-6

GROUP_Q = ATTN_GROUP * ATTN_HEAD_DIM
GROUP_KV = 2 * ATTN_HEAD_DIM
GROUP_COLS = GROUP_Q + GROUP_KV + GROUP_Q
HEAD_COLS = 5 * HGRN_DK

Q_TILE = 256
VT_ROWS = ATTN_HEAD_DIM + 16
KEY_BLOCK = 256
HGRN_CHUNK = 64
HGRN_MID = HGRN_CHUNK // 2 - 1
HGRN_SUPER = 256
PROJ_BLOCKS = 4
HGRN_HEADS_PER_STEP = 2
HGRN_SAFE_EXP2 = 90.0
LOG2_E = math.log2(math.e)
MERGE_TILE = 512

VMEM_LIMIT = 56 * 1024 * 1024
ATTN_VMEM_LIMIT = 62 * 1024 * 1024


def _dot(a, b):
    return jnp.dot(a, b, preferred_element_type=F32)


def _dot_nt(a, b):
    return lax.dot_general(a, b, (((1,), (1,)), ((), ())), preferred_element_type=F32)


def _dot_tn(a, b):
    return lax.dot_general(a, b, (((0,), (0,)), ((), ())), preferred_element_type=F32)


def _silu(x):
    return x * jax.nn.sigmoid(x)


def _split2(x):
    hi = x.astype(BF16)
    lo = (x - hi.astype(F32)).astype(BF16)
    return hi, lo


def _prenorm_kernel(x_ref, w_ref, h_ref):
    x = x_ref[...]
    ms = jnp.mean(x * x, axis=-1, keepdims=True)
    h_ref[...] = (x * lax.rsqrt(ms + NORM_EPS) * w_ref[...]).astype(BF16)


def _prenorm(x2, w):
    n, d = x2.shape
    tm = MERGE_TILE
    return pl.pallas_call(
        _prenorm_kernel,
        grid=(n // tm,),
        in_specs=[pl.BlockSpec((tm, d), lambda i: (i, 0)), pl.BlockSpec((1, d), lambda i: (0, 0))],
        out_specs=pl.BlockSpec((tm, d), lambda i: (i, 0)),
        out_shape=jax.ShapeDtypeStruct((n, d), BF16),
        compiler_params=pltpu.CompilerParams(dimension_semantics=("arbitrary",)),
    )(x2, w.reshape(1, d))


def _rope_partner(x):
    width = x.shape[-1]
    lane = lax.broadcasted_iota(jnp.int32, x.shape, 1)
    first = (lane & (ROPE_AXIS_DIM - 1)) < (ROPE_AXIS_DIM // 2)
    return jnp.where(first, pltpu.roll(x, width - ROPE_AXIS_DIM // 2, 1), pltpu.roll(x, ROPE_AXIS_DIM // 2, 1))


def _attn_kernel(h_ref, w_ref, cos_ref, sin_ref, cos_t_ref, sin_t_ref, qw_ref, kw_ref, out_ref,
                 qt_ref, k_ref, vt_ref, gate_ref, s_ref, m_ref):
    seq = h_ref.shape[1]
    n_tiles = seq // Q_TILE
    n_groups = w_ref.shape[0]
    hd = ATTN_HEAD_DIM

    q_scale = qw_ref[...] * (hd ** -0.5 * LOG2_E)
    r = lax.broadcasted_iota(jnp.int32, (GROUP_Q, GROUP_Q), 0)
    c = lax.broadcasted_iota(jnp.int32, (GROUP_Q, GROUP_Q), 1)
    same_head = jnp.where((r ^ c) < hd, 1.0, 0.0).astype(BF16)
    rb = seq // PROJ_BLOCKS
    is_k = lax.broadcasted_iota(jnp.int32, (rb, GROUP_KV), 1) < hd
    for par in range(2):
        vt_ref[par, hd:, :] = jnp.ones((VT_ROWS - hd, seq), BF16)

    def project(g):
        par = g % 2
        w = w_ref[g]
        for blk in range(PROJ_BLOCKS):
            rows = slice(blk * rb, (blk + 1) * rb)
            proj = _dot(h_ref[0, rows, :], w)
            q = proj[:, :GROUP_Q]
            kv = proj[:, GROUP_Q:GROUP_Q + GROUP_KV]
            gate_ref[par, rows, :] = _silu(proj[:, GROUP_Q + GROUP_KV:])

            hi, lo = _split2(q * q)
            ssq = _dot(hi, same_head) + _dot(lo, same_head)
            qn_t = (q * lax.rsqrt(ssq * (1.0 / hd) + NORM_EPS) * q_scale).T
            half = ROPE_AXIS_DIM // 2
            parts = []
            for grp in range(GROUP_Q // ROPE_AXIS_DIM):
                base = grp * ROPE_AXIS_DIM
                parts += [qn_t[base + half:base + 2 * half], qn_t[base:base + half]]
            partner_t = jnp.concatenate(parts, axis=0)
            cos_t = jnp.concatenate([cos_t_ref[:, rows]] * ATTN_GROUP, axis=0)
            sin_t = jnp.concatenate([sin_t_ref[:, rows]] * ATTN_GROUP, axis=0)
            qr_t = (qn_t * cos_t + partner_t * sin_t).astype(BF16)
            for t in range(rb // Q_TILE):
                for j in range(ATTN_GROUP):
                    qt_ref[par, blk * (rb // Q_TILE) + t, :, j * Q_TILE:(j + 1) * Q_TILE] = \
                        qr_t[j * hd:(j + 1) * hd, t * Q_TILE:(t + 1) * Q_TILE]

            cos = cos_ref[rows, :]
            sin = sin_ref[rows, :]
            ssk = jnp.sum(jnp.where(is_k, kv * kv, 0.0), axis=-1, keepdims=True)
            kn = kv * lax.rsqrt(ssk * (1.0 / hd) + NORM_EPS) * kw_ref[...]
            kr = kn * cos + _rope_partner(kn) * sin
            k_ref[par, rows, :] = kr[:, :hd].astype(BF16)
            vt_ref[par, :hd, rows] = kv.T[hd:, :].astype(BF16)

    def step(g, nxt, cur):
        par = g % 2
        if cur is not None:
            m = m_ref[cur[1]]
        acc = None
        m_new = None
        for kb in range(seq // KEY_BLOCK):
            rows = slice(kb * KEY_BLOCK, (kb + 1) * KEY_BLOCK)
            if nxt is not None:
                sc = _dot(k_ref[par, rows, :], qt_ref[par, nxt[0]])
                s_ref[nxt[1], rows, :] = sc
                cm = jnp.max(sc, axis=0, keepdims=True)
                m_new = cm if m_new is None else jnp.maximum(m_new, cm)
            if cur is not None:
                p = jnp.exp2(s_ref[cur[1], rows, :] - m).astype(BF16)
                part = _dot(vt_ref[par, :, rows], p)
                acc = part if acc is None else acc + part
        if nxt is not None:
            m_ref[nxt[1]] = m_new
        if cur is not None:
            o = acc[:hd] / acc[hd:hd + 1]
            o = jnp.concatenate([o[:, j * Q_TILE:(j + 1) * Q_TILE] for j in range(ATTN_GROUP)], axis=0).T
            col = pl.multiple_of(cur[0] * Q_TILE, Q_TILE)
            out_ref[0, pl.ds(col, Q_TILE), g * GROUP_Q:(g + 1) * GROUP_Q] = \
                (o * gate_ref[par, pl.ds(col, Q_TILE), :]).astype(BF16)

    project(0)
    for g in range(n_groups):
        step(g, (0, 0), None)

        def pair(j, carry, g=g):
            i = 2 * j
            step(g, (i + 1, 1), (i, 0))
            step(g, (i + 2, 0), (i + 1, 1))
            return carry

        lax.fori_loop(0, n_tiles // 2 - 1, pair, 0)
        step(g, (n_tiles - 1, 1), (n_tiles - 2, 0))
        if g + 1 < n_groups:
            project(g + 1)
        step(g, None, (n_tiles - 1, 1))


def _attention(h, w_a, cos, sin, cos_t, sin_t, qw, kw):
    b, s, d = h.shape
    g = w_a.shape[0]
    whole2 = lambda i: (0, 0)
    return pl.pallas_call(
        _attn_kernel,
        grid=(b,),
        in_specs=[
            pl.BlockSpec((1, s, d), lambda i: (i, 0, 0)),
            pl.BlockSpec((g, d, GROUP_COLS), lambda i: (0, 0, 0)),
            pl.BlockSpec((s, GROUP_KV), whole2),
            pl.BlockSpec((s, GROUP_KV), whole2),
            pl.BlockSpec((ATTN_HEAD_DIM, s), whole2),
            pl.BlockSpec((ATTN_HEAD_DIM, s), whole2),
            pl.BlockSpec((1, GROUP_Q), whole2),
            pl.BlockSpec((1, GROUP_KV), whole2),
        ],
        out_specs=pl.BlockSpec((1, s, g * GROUP_Q), lambda i: (i, 0, 0)),
        out_shape=jax.ShapeDtypeStruct((b, s, g * GROUP_Q), BF16),
        scratch_shapes=[
            pltpu.VMEM((2, s // Q_TILE, ATTN_HEAD_DIM, ATTN_GROUP * Q_TILE), BF16),
            pltpu.VMEM((2, s, ATTN_HEAD_DIM), BF16),
            pltpu.VMEM((2, VT_ROWS, s), BF16),
            pltpu.VMEM((2, s, GROUP_Q), F32),
            pltpu.VMEM((2, s, ATTN_GROUP * Q_TILE), F32),
            pltpu.VMEM((2, 1, ATTN_GROUP * Q_TILE), F32),
        ],
        compiler_params=pltpu.CompilerParams(
            dimension_semantics=("arbitrary",), vmem_limit_bytes=ATTN_VMEM_LIMIT),
    )(h, w_a, cos, sin, cos_t, sin_t, qw, kw)


def _forget_gate(z, lb):
    u = jnp.exp(-jnp.abs(z))
    t = 1.0 + u
    log_sig = jnp.minimum(z, 0.0) - jnp.log(t)
    sig_neg = jnp.where(z >= 0.0, u, 1.0) / t
    a = jnp.log(lb)
    c = jnp.log(1.0 - lb) + log_sig
    mx = jnp.maximum(a, c)
    g = mx + jnp.log(1.0 + jnp.exp(jnp.minimum(a, c) - mx))
    return g * LOG2_E, (1.0 - lb) * sig_neg


def _hgrn_kernel(layer, h_ref, w_ref, lbp_ref, gw_ref, out_ref, q_ref, v_ref, k_ref, b_ref, oi_ref, gate_ref):
    seq = h_ref.shape[1]
    ch = HGRN_CHUNK
    sup = HGRN_SUPER
    n_chunks = seq // ch
    dk = HGRN_DK
    rb = seq // PROJ_BLOCKS
    blk_chunks = rb // ch
    heads = range(HGRN_HEADS_PER_STEP)

    row = lax.broadcasted_iota(jnp.int32, (sup, sup), 0)
    colm = lax.broadcasted_iota(jnp.int32, (sup, sup), 1)
    same_chunk = (row ^ colm) < ch
    masks = (same_chunk & (colm <= row), same_chunk & (colm >= row))
    tri = [jnp.where(mk, 1.0, 0.0).astype(BF16) for mk in masks]

    lbs = []
    for d in range(2):
        p = lbp_ref[d]
        depth = p.shape[0]
        pm = p[0:1]
        for i in range(1, depth):
            pm = jnp.maximum(pm, p[i:i + 1])
        ex = [jnp.exp(p[i:i + 1] - pm) for i in range(depth)]
        tot = ex[0]
        for i in range(1, depth):
            tot = tot + ex[i]
        first = ex[0] / tot
        cum = first
        for i in range(1, layer + 1):
            cum = cum + ex[i] / tot
        lbs.append(cum - first)

    w = w_ref[0]
    decay = [[[], []] for _ in heads]
    gs = [[] for _ in heads]
    qh2 = [[] for _ in heads]
    viol = jnp.zeros((1, 1, dk), F32)
    for blk in range(PROJ_BLOCKS):
        rows = slice(blk * rb, (blk + 1) * rb)
        proj_all = _dot(h_ref[0, rows, :], w)
        for hh in heads:
            proj = proj_all[:, hh * HEAD_COLS:(hh + 1) * HEAD_COLS]
            q = _silu(proj[:, :dk]) * (dk ** -0.5)
            v = proj[:, 3 * dk:4 * dk]
            gate_ref[hh, rows, :] = _silu(proj[:, 4 * dk:])
            q_ref[hh, rows, :] = q
            v_ref[hh, rows, :] = v
            vb = v.astype(BF16)
            q3 = q.reshape(blk_chunks, ch, dk)

            qe, ke, qh, kd = [], [], [], []
            for d in range(2):
                backward = d == 1
                g, k = _forget_gate(proj[:, (1 + d) * dk:(2 + d) * dk],
                                    lbs[d][:, hh * dk:(hh + 1) * dk])
                k_ref[hh, d, rows, :] = k
                g2 = jnp.concatenate(_split2(g), axis=1)
                bs = []
                for i in range(rb // sup):
                    cs = _dot(tri[d], g2[i * sup:(i + 1) * sup])
                    bs.append(cs[:, :dk] + cs[:, dk:])
                b = jnp.concatenate(bs, axis=0)
                b_ref[hh, d, rows, :] = b

                edge = 0 if backward else ch - 1
                mid = ch - 1 - HGRN_MID if backward else HGRN_MID
                b3 = b.reshape(blk_chunks, ch, dk)
                k3 = k.reshape(blk_chunks, ch, dk)
                b_edge = b3[:, edge:edge + 1, :]
                b_mid = b3[:, mid:mid + 1, :]
                rel = b3 - b_mid
                ends = jnp.maximum(jnp.abs(b3[:, 0:1, :] - b_mid), jnp.abs(b3[:, ch - 1:ch, :] - b_mid))
                viol = jnp.maximum(viol, jnp.max(ends, axis=0, keepdims=True))
                qe.append((q3 * jnp.exp2(rel)).astype(BF16).reshape(rb, dk))
                ke.append((k3 * jnp.exp2(-rel)).astype(BF16).reshape(rb, dk))
                qh.append((q3 * jnp.exp2(b3)).astype(BF16).reshape(rb, dk))
                kd.append((k3 * jnp.exp2(b_edge - b3)).astype(BF16).reshape(rb, dk))
                e = jnp.exp2(b_edge)
                decay[hh][d] += [e[c] for c in range(blk_chunks)]

            for i in range(rb // sup):
                sl = slice(i * sup, (i + 1) * sup)
                a = jnp.where(masks[0], _dot_nt(qe[0][sl], ke[0][sl]), 0.0) \
                    + jnp.where(masks[1], _dot_nt(qe[1][sl], ke[1][sl]), 0.0)
                oi_ref[hh, blk * rb + i * sup:blk * rb + (i + 1) * sup, :] = _dot(a.astype(BF16), vb[sl])
            kd2 = jnp.concatenate(kd, axis=1)
            gs[hh] += [_dot_tn(vb[c * ch:(c + 1) * ch], kd2[c * ch:(c + 1) * ch]) for c in range(blk_chunks)]
            qh2[hh].append(jnp.concatenate(qh, axis=1))

    o_inter = []
    for hh in heads:
        entering = [[None] * n_chunks, [None] * n_chunks]
        for d in range(2):
            state = jnp.zeros((HGRN_DV, dk), F32)
            order = range(n_chunks - 1, -1, -1) if d == 1 else range(n_chunks)
            for c in order:
                entering[d][c] = state
                state = state * decay[hh][d][c] + gs[hh][c][:, d * dk:(d + 1) * dk]
        inter = []
        for c in range(n_chunks):
            st = jnp.concatenate([entering[0][c], entering[1][c]], axis=1).astype(BF16)
            lc = c % blk_chunks
            inter.append(_dot_nt(qh2[hh][c // blk_chunks][lc * ch:(lc + 1) * ch], st))
        o_inter.append(jnp.concatenate(inter, axis=0))

    @pl.when(jnp.max(viol) > HGRN_SAFE_EXP2)
    def _():
        def one_row(t, carry):
            t0 = pl.multiple_of((t // ch) * ch, ch)
            s_idx = t0 + lax.broadcasted_iota(jnp.int32, (ch, 1), 0)
            for hh in heads:
                acc = jnp.zeros((1, HGRN_DV), F32)
                for d in range(2):
                    bc = b_ref[hh, d, pl.ds(t0, ch), :]
                    bt = b_ref[hh, d, pl.ds(t, 1), :]
                    valid = (s_idx >= t) if d == 1 else (s_idx <= t)
                    dec = jnp.where(valid, jnp.exp2(jnp.minimum(bt - bc, 0.0)), 0.0)
                    wgt = dec * (q_ref[hh, pl.ds(t, 1), :] * k_ref[hh, d, pl.ds(t0, ch), :])
                    sc = jnp.sum(wgt, axis=1, keepdims=True)
                    acc = acc + jnp.sum(sc * v_ref[hh, pl.ds(t0, ch), :], axis=0, keepdims=True)
                oi_ref[hh, pl.ds(t, 1), :] = acc
            return carry

        lax.fori_loop(0, seq, one_row, 0)

    for hh in heads:
        o = o_inter[hh] + oi_ref[hh]
        ms = jnp.mean(o * o, axis=-1, keepdims=True)
        out_ref[0, :, hh * HGRN_DV:(hh + 1) * HGRN_DV] = \
            (o * lax.rsqrt(ms + NORM_EPS) * gw_ref[...] * gate_ref[hh]).astype(BF16)


def _hgrn(h, w_h, lb_params, gw, layer):
    b, s, d = h.shape
    hps = HGRN_HEADS_PER_STEP
    n_steps = w_h.shape[0]
    ndir, depth, _ = lb_params.shape
    return pl.pallas_call(
        functools.partial(_hgrn_kernel, layer),
        grid=(b, n_steps),
        in_specs=[
            pl.BlockSpec((1, s, d), lambda i, j: (i, 0, 0)),
            pl.BlockSpec((1, d, hps * HEAD_COLS), lambda i, j: (j, 0, 0)),
            pl.BlockSpec((ndir, depth, hps * HGRN_DK), lambda i, j: (0, 0, j)),
            pl.BlockSpec((1, HGRN_DV), lambda i, j: (0, 0)),
        ],
        out_specs=pl.BlockSpec((1, s, hps * HGRN_DV), lambda i, j: (i, 0, j)),
        out_shape=jax.ShapeDtypeStruct((b, s, n_steps * hps * HGRN_DV), BF16),
        scratch_shapes=[
            pltpu.VMEM((hps, s, HGRN_DK), F32),
            pltpu.VMEM((hps, s, HGRN_DV), F32),
            pltpu.VMEM((hps, 2, s, HGRN_DK), F32),
            pltpu.VMEM((hps, 2, s, HGRN_DK), F32),
            pltpu.VMEM((hps, s, HGRN_DV), F32),
            pltpu.VMEM((hps, s, HGRN_DV), F32),
        ],
        compiler_params=pltpu.CompilerParams(
            dimension_semantics=("arbitrary", "arbitrary"), vmem_limit_bytes=VMEM_LIMIT),
    )(h, w_h, lb_params, gw)


def _merge_kernel(last, x_ref, h_ref, ya_ref, yh_ref, wm_ref, wba_ref, wbh_ref, wo_ref, nw_ref, *out_refs):
    d = x_ref.shape[1]
    m = _dot(h_ref[...], wm_ref[...])
    merged = jax.nn.sigmoid(m[:, :d]) * _dot(ya_ref[...], wba_ref[...]) \
        + jax.nn.sigmoid(m[:, d:]) * _dot(yh_ref[...], wbh_ref[...])
    x_new = x_ref[...] + _dot(merged.astype(BF16), wo_ref[...])
    ms = jnp.mean(x_new * x_new, axis=-1, keepdims=True)
    normed = x_new * lax.rsqrt(ms + NORM_EPS) * nw_ref[...]
    if last:
        out_refs[0][...] = normed
    else:
        out_refs[0][...] = x_new
        out_refs[1][...] = normed.astype(BF16)


def _merge(x2, h2, ya2, yh2, wm, wba, wbh, wo, nw, last):
    n, d = x2.shape
    tm = MERGE_TILE
    tile = lambda i: (i, 0)
    whole = lambda i: (0, 0)
    if last:
        out_shape = (jax.ShapeDtypeStruct((n, d), F32),)
        out_specs = (pl.BlockSpec((tm, d), tile),)
    else:
        out_shape = (jax.ShapeDtypeStruct((n, d), F32), jax.ShapeDtypeStruct((n, d), BF16))
        out_specs = (pl.BlockSpec((tm, d), tile), pl.BlockSpec((tm, d), tile))
    return pl.pallas_call(
        functools.partial(_merge_kernel, last),
        grid=(n // tm,),
        in_specs=[
            pl.BlockSpec((tm, d), tile), pl.BlockSpec((tm, d), tile),
            pl.BlockSpec((tm, d), tile), pl.BlockSpec((tm, d), tile),
            pl.BlockSpec((d, 2 * d), whole), pl.BlockSpec((d, d), whole),
            pl.BlockSpec((d, d), whole), pl.BlockSpec((d, d), whole),
            pl.BlockSpec((1, d), whole),
        ],
        out_specs=out_specs,
        out_shape=out_shape,
        compiler_params=pltpu.CompilerParams(
            dimension_semantics=("arbitrary",), vmem_limit_bytes=VMEM_LIMIT),
    )(x2, h2, ya2, yh2, wm, wba, wbh, wo, nw.reshape(1, d))


def _rope_tables(seq):
    rows = seq // GRID_W
    row = jnp.repeat(jnp.arange(rows), GRID_W).astype(F32)
    col = jnp.tile(jnp.arange(GRID_W), rows).astype(F32)
    inv_freq = ROPE_THETA ** (-jnp.arange(0, ROPE_AXIS_DIM, 2, dtype=F32) / ROPE_AXIS_DIM)
    ang_r = row[:, None] * inv_freq
    ang_c = col[:, None] * inv_freq
    cr, sr, cc, sc = jnp.cos(ang_r), jnp.sin(ang_r), jnp.cos(ang_c), jnp.sin(ang_c)
    cos = jnp.concatenate([cr, cr, cc, cc], axis=1)
    sin = jnp.concatenate([-sr, sr, -sc, sc], axis=1)
    return jnp.tile(cos, (1, 2)), jnp.tile(sin, (1, 2)), cos.T, sin.T


def _layer_weights(w_in_l, d_model):
    aw = ATTN_HEADS * ATTN_HEAD_DIM
    kvw = ATTN_KV_HEADS * ATTN_HEAD_DIM
    hw = HGRN_HEADS * HGRN_DK
    sizes = (aw, kvw, kvw, aw, hw, hw, hw, hw, hw, d_model, d_model)
    offs = [0]
    for sz in sizes:
        offs.append(offs[-1] + sz)
    parts = [w_in_l[:, offs[i]:offs[i + 1]].astype(BF16) for i in range(len(sizes))]
    wq, wk, wv, wag, whq, whf, whb, whi, whg, wma, wmh = parts
    g = ATTN_KV_HEADS
    w_a = jnp.concatenate([
        wq.reshape(d_model, g, GROUP_Q), wk.reshape(d_model, g, ATTN_HEAD_DIM),
        wv.reshape(d_model, g, ATTN_HEAD_DIM), wag.reshape(d_model, g, GROUP_Q)], axis=2)
    w_a = w_a.transpose(1, 0, 2)
    w_h = jnp.concatenate([w.reshape(d_model, HGRN_HEADS, HGRN_DK) for w in (whq, whf, whb, whi, whg)], axis=2)
    w_h = w_h.reshape(d_model, HGRN_HEADS // HGRN_HEADS_PER_STEP, HGRN_HEADS_PER_STEP * HEAD_COLS)
    w_h = w_h.transpose(1, 0, 2)
    w_m = jnp.concatenate([wma, wmh], axis=1)
    return w_a, w_h, w_m


def kernel(x, w_in, norm_w, q_norm_w, k_norm_w, hgrn_lower_bounds, hgrn_norm_w, w_branch_attn, w_branch_hgrn,
           w_out, final_norm_w):
    b, s, d = x.shape
    depth = w_in.shape[0]
    assert s % Q_TILE == 0 and s % HGRN_CHUNK == 0 and s % GRID_W == 0 and (b * s) % MERGE_TILE == 0
    cos, sin, cos_t, sin_t = _rope_tables(s)
    x2 = x.reshape(b * s, d)
    h2 = _prenorm(x2, norm_w[0])
    out = None
    for layer in range(depth):
        w_a, w_h, w_m = _layer_weights(w_in[layer], d)
        h3 = h2.reshape(b, s, d)
        qw = jnp.tile(q_norm_w[layer], ATTN_GROUP).reshape(1, GROUP_Q)
        kw = jnp.concatenate([k_norm_w[layer], jnp.ones((ATTN_HEAD_DIM,), F32)]).reshape(1, GROUP_KV)
        ya = _attention(h3, w_a, cos, sin, cos_t, sin_t, qw, kw)
        yh = _hgrn(h3, w_h, hgrn_lower_bounds.astype(F32), hgrn_norm_w[layer].reshape(1, HGRN_DV), layer)
        last = layer == depth - 1
        nw = final_norm_w if last else norm_w[layer + 1]
        res = _merge(x2, h2, ya.reshape(b * s, -1), yh.reshape(b * s, -1), w_m,
                     w_branch_attn[layer].astype(BF16), w_branch_hgrn[layer].astype(BF16),
                     w_out[layer].astype(BF16), nw, last)
        if last:
            out = res[0]
        else:
            x2, h2 = res
    return out.reshape(b, s, d)
```

```python
import functools
import math

import jax
import jax.numpy as jnp
from jax import lax
from jax.experimental import pallas as pl
from jax.experimental.pallas import tpu as pltpu

F32 = jnp.float32
BF16 = jnp.bfloat16

GRID_W = 64
ATTN_HEADS = 16
ATTN_KV_HEADS = 4
ATTN_HEAD_DIM = 64
ATTN_GROUP = ATTN_HEADS // ATTN_KV_HEADS
ROPE_AXIS_DIM = ATTN_HEAD_DIM // 2
ROPE_THETA = 10000.0
HGRN_HEADS = 8
HGRN_DK = 128
HGRN_DV = 128
NORM_EPS = 1e-6

GROUP_Q = ATTN_GROUP * ATTN_HEAD_DIM
GROUP_KV = 2 * ATTN_HEAD_DIM
GROUP_COLS = GROUP_Q + GROUP_KV + GROUP_Q
HEAD_COLS = 5 * HGRN_DK

Q_TILE = 256
VT_ROWS = ATTN_HEAD_DIM + 16
KEY_BLOCK = 256
HGRN_CHUNK = 64
HGRN_MID = HGRN_CHUNK // 2 - 1
HGRN_SUPER = 256
PROJ_BLOCKS = 4
HGRN_HEADS_PER_STEP = 2
HGRN_SAFE_EXP2 = 90.0
LOG2_E = math.log2(math.e)
MERGE_TILE = 1024

VMEM_LIMIT = 56 * 1024 * 1024


def _dot(a, b):
    return jnp.dot(a, b, preferred_element_type=F32)


def _dot_nt(a, b):
    return lax.dot_general(a, b, (((1,), (1,)), ((), ())), preferred_element_type=F32)


def _dot_tn(a, b):
    return lax.dot_general(a, b, (((0,), (0,)), ((), ())), preferred_element_type=F32)


def _silu(x):
    return x * jax.nn.sigmoid(x)


def _split2(x):
    hi = x.astype(BF16)
    lo = (x - hi.astype(F32)).astype(BF16)
    return hi, lo


def _prenorm_kernel(x_ref, w_ref, h_ref):
    x = x_ref[...]
    ms = jnp.mean(x * x, axis=-1, keepdims=True)
    h_ref[...] = (x * lax.rsqrt(ms + NORM_EPS) * w_ref[...]).astype(BF16)


def _prenorm(x2, w):
    n, d = x2.shape
    tm = MERGE_TILE
    return pl.pallas_call(
        _prenorm_kernel,
        grid=(n // tm,),
        in_specs=[pl.BlockSpec((tm, d), lambda i: (i, 0)), pl.BlockSpec((1, d), lambda i: (0, 0))],
        out_specs=pl.BlockSpec((tm, d), lambda i: (i, 0)),
        out_shape=jax.ShapeDtypeStruct((n, d), BF16),
        compiler_params=pltpu.CompilerParams(dimension_semantics=("arbitrary",)),
    )(x2, w.reshape(1, d))


def _rope_partner(x):
    width = x.shape[-1]
    lane = lax.broadcasted_iota(jnp.int32, x.shape, 1)
    first = (lane & (ROPE_AXIS_DIM - 1)) < (ROPE_AXIS_DIM // 2)
    return jnp.where(first, pltpu.roll(x, width - ROPE_AXIS_DIM // 2, 1), pltpu.roll(x, ROPE_AXIS_DIM // 2, 1))


def _attn_kernel(h_ref, w_ref, cos_ref, sin_ref, cos_t_ref, sin_t_ref, qw_ref, kw_ref, out_ref,
                 qt_ref, k_ref, vt_ref, gate_ref, s_ref, m_ref):
    seq = h_ref.shape[1]
    n_tiles = seq // Q_TILE
    hd = ATTN_HEAD_DIM

    w = w_ref[...]
    q_scale = qw_ref[...] * (hd ** -0.5 * LOG2_E)
    r = lax.broadcasted_iota(jnp.int32, (GROUP_Q, GROUP_Q), 0)
    c = lax.broadcasted_iota(jnp.int32, (GROUP_Q, GROUP_Q), 1)
    same_head = jnp.where((r ^ c) < hd, 1.0, 0.0).astype(BF16)
    rb = seq // PROJ_BLOCKS
    is_k = lax.broadcasted_iota(jnp.int32, (rb, GROUP_KV), 1) < hd
    vt_ref[hd:, :] = jnp.ones((VT_ROWS - hd, seq), BF16)

    for blk in range(PROJ_BLOCKS):
        rows = slice(blk * rb, (blk + 1) * rb)
        proj = _dot(h_ref[0, rows, :], w)
        q = proj[:, :GROUP_Q]
        kv = proj[:, GROUP_Q:GROUP_Q + GROUP_KV]
        gate_ref[rows, :] = _silu(proj[:, GROUP_Q + GROUP_KV:])

        hi, lo = _split2(q * q)
        ssq = _dot(hi, same_head) + _dot(lo, same_head)
        qn_t = (q * lax.rsqrt(ssq * (1.0 / hd) + NORM_EPS) * q_scale).T
        half = ROPE_AXIS_DIM // 2
        parts = []
        for grp in range(GROUP_Q // ROPE_AXIS_DIM):
            base = grp * ROPE_AXIS_DIM
            parts += [qn_t[base + half:base + 2 * half], qn_t[base:base + half]]
        partner_t = jnp.concatenate(parts, axis=0)
        cos_t = jnp.concatenate([cos_t_ref[:, rows]] * ATTN_GROUP, axis=0)
        sin_t = jnp.concatenate([sin_t_ref[:, rows]] * ATTN_GROUP, axis=0)
        qr_t = (qn_t * cos_t + partner_t * sin_t).astype(BF16)
        for t in range(rb // Q_TILE):
            for j in range(ATTN_GROUP):
                qt_ref[blk * (rb // Q_TILE) + t, :, j * Q_TILE:(j + 1) * Q_TILE] = \
                    qr_t[j * hd:(j + 1) * hd, t * Q_TILE:(t + 1) * Q_TILE]

        cos = cos_ref[rows, :]
        sin = sin_ref[rows, :]
        ssk = jnp.sum(jnp.where(is_k, kv * kv, 0.0), axis=-1, keepdims=True)
        kn = kv * lax.rsqrt(ssk * (1.0 / hd) + NORM_EPS) * kw_ref[...]
        kr = kn * cos + _rope_partner(kn) * sin
        k_ref[rows, :] = kr[:, :hd].astype(BF16)
        vt_ref[:hd, rows] = kv.T[hd:, :].astype(BF16)

    def step(nxt, cur):
        if cur is not None:
            m = m_ref[cur[1]]
        acc = None
        m_new = None
        for kb in range(seq // KEY_BLOCK):
            rows = slice(kb * KEY_BLOCK, (kb + 1) * KEY_BLOCK)
            if nxt is not None:
                sc = _dot(k_ref[rows, :], qt_ref[nxt[0]])
                s_ref[nxt[1], rows, :] = sc
                cm = jnp.max(sc, axis=0, keepdims=True)
                m_new = cm if m_new is None else jnp.maximum(m_new, cm)
            if cur is not None:
                p = jnp.exp2(s_ref[cur[1], rows, :] - m).astype(BF16)
                part = _dot(vt_ref[:, rows], p)
                acc = part if acc is None else acc + part
        if nxt is not None:
            m_ref[nxt[1]] = m_new
        if cur is not None:
            o = acc[:hd] / acc[hd:hd + 1]
            o = jnp.concatenate([o[:, j * Q_TILE:(j + 1) * Q_TILE] for j in range(ATTN_GROUP)], axis=0).T
            col = pl.multiple_of(cur[0] * Q_TILE, Q_TILE)
            out_ref[0, pl.ds(col, Q_TILE), :] = (o * gate_ref[pl.ds(col, Q_TILE), :]).astype(BF16)

    step((0, 0), None)

    def pair(j, carry):
        i = 2 * j
        step((i + 1, 1), (i, 0))
        step((i + 2, 0), (i + 1, 1))
        return carry

    lax.fori_loop(0, n_tiles // 2 - 1, pair, 0)
    step((n_tiles - 1, 1), (n_tiles - 2, 0))
    step(None, (n_tiles - 1, 1))


def _attention(h, w_a, cos, sin, cos_t, sin_t, qw, kw):
    b, s, d = h.shape
    g = w_a.shape[1] // GROUP_COLS
    return pl.pallas_call(
        _attn_kernel,
        grid=(b, g),
        in_specs=[
            pl.BlockSpec((1, s, d), lambda i, j: (i, 0, 0)),
            pl.BlockSpec((d, GROUP_COLS), lambda i, j: (0, j)),
            pl.BlockSpec((s, GROUP_KV), lambda i, j: (0, 0)),
            pl.BlockSpec((s, GROUP_KV), lambda i, j: (0, 0)),
            pl.BlockSpec((ATTN_HEAD_DIM, s), lambda i, j: (0, 0)),
            pl.BlockSpec((ATTN_HEAD_DIM, s), lambda i, j: (0, 0)),
            pl.BlockSpec((1, GROUP_Q), lambda i, j: (0, 0)),
            pl.BlockSpec((1, GROUP_KV), lambda i, j: (0, 0)),
        ],
        out_specs=pl.BlockSpec((1, s, GROUP_Q), lambda i, j: (i, 0, j)),
        out_shape=jax.ShapeDtypeStruct((b, s, g * GROUP_Q), BF16),
        scratch_shapes=[
            pltpu.VMEM((s // Q_TILE, ATTN_HEAD_DIM, ATTN_GROUP * Q_TILE), BF16),
            pltpu.VMEM((s, ATTN_HEAD_DIM), BF16),
            pltpu.VMEM((VT_ROWS, s), BF16),
            pltpu.VMEM((s, GROUP_Q), F32),
            pltpu.VMEM((2, s, ATTN_GROUP * Q_TILE), F32),
            pltpu.VMEM((2, 1, ATTN_GROUP * Q_TILE), F32),
        ],
        compiler_params=pltpu.CompilerParams(
            dimension_semantics=("arbitrary", "arbitrary"), vmem_limit_bytes=VMEM_LIMIT),
    )(h, w_a, cos, sin, cos_t, sin_t, qw, kw)


def _forget_gate(z, lb):
    u = jnp.exp(-jnp.abs(z))
    t = 1.0 + u
    log_sig = jnp.minimum(z, 0.0) - jnp.log(t)
    sig_neg = jnp.where(z >= 0.0, u, 1.0) / t
    a = jnp.log(lb)
    c = jnp.log(1.0 - lb) + log_sig
    mx = jnp.maximum(a, c)
    g = mx + jnp.log(1.0 + jnp.exp(jnp.minimum(a, c) - mx))
    return g * LOG2_E, (1.0 - lb) * sig_neg


def _hgrn_kernel(layer, h_ref, w_ref, lbp_ref, gw_ref, out_ref, q_ref, v_ref, k_ref, b_ref, oi_ref, gate_ref):
    seq = h_ref.shape[1]
    ch = HGRN_CHUNK
    sup = HGRN_SUPER
    n_chunks = seq // ch
    dk = HGRN_DK
    rb = seq // PROJ_BLOCKS
    blk_chunks = rb // ch
    heads = range(HGRN_HEADS_PER_STEP)

    row = lax.broadcasted_iota(jnp.int32, (sup, sup), 0)
    colm = lax.broadcasted_iota(jnp.int32, (sup, sup), 1)
    same_chunk = (row ^ colm) < ch
    masks = (same_chunk & (colm <= row), same_chunk & (colm >= row))
    tri = [jnp.where(mk, 1.0, 0.0).astype(BF16) for mk in masks]

    lbs = []
    for d in range(2):
        p = lbp_ref[d]
        depth = p.shape[0]
        pm = p[0:1]
        for i in range(1, depth):
            pm = jnp.maximum(pm, p[i:i + 1])
        ex = [jnp.exp(p[i:i + 1] - pm) for i in range(depth)]
        tot = ex[0]
        for i in range(1, depth):
            tot = tot + ex[i]
        first = ex[0] / tot
        cum = first
        for i in range(1, layer + 1):
            cum = cum + ex[i] / tot
        lbs.append(cum - first)

    w = w_ref[...]
    decay = [[[], []] for _ in heads]
    gs = [[] for _ in heads]
    qh2 = [[] for _ in heads]
    viol = jnp.zeros((1, 1, dk), F32)
    for blk in range(PROJ_BLOCKS):
        rows = slice(blk * rb, (blk + 1) * rb)
        proj_all = _dot(h_ref[0, rows, :], w)
        for hh in heads:
            proj = proj_all[:, hh * HEAD_COLS:(hh + 1) * HEAD_COLS]
            q = _silu(proj[:, :dk]) * (dk ** -0.5)
            v = proj[:, 3 * dk:4 * dk]
            gate_ref[hh, rows, :] = _silu(proj[:, 4 * dk:])
            q_ref[hh, rows, :] = q
            v_ref[hh, rows, :] = v
            vb = v.astype(BF16)
            q3 = q.reshape(blk_chunks, ch, dk)

            qe, ke, qh, kd = [], [], [], []
            for d in range(2):
                backward = d == 1
                g, k = _forget_gate(proj[:, (1 + d) * dk:(2 + d) * dk],
                                    lbs[d][:, hh * dk:(hh + 1) * dk])
                k_ref[hh, d, rows, :] = k
                g2 = jnp.concatenate(_split2(g), axis=1)
                bs = []
                for i in range(rb // sup):
                    cs = _dot(tri[d], g2[i * sup:(i + 1) * sup])
                    bs.append(cs[:, :dk] + cs[:, dk:])
                b = jnp.concatenate(bs, axis=0)
                b_ref[hh, d, rows, :] = b

                edge = 0 if backward else ch - 1
                mid = ch - 1 - HGRN_MID if backward else HGRN_MID
                b3 = b.reshape(blk_chunks, ch, dk)
                k3 = k.reshape(blk_chunks, ch, dk)
                b_edge = b3[:, edge:edge + 1, :]
                b_mid = b3[:, mid:mid + 1, :]
                rel = b3 - b_mid
                ends = jnp.maximum(jnp.abs(b3[:, 0:1, :] - b_mid), jnp.abs(b3[:, ch - 1:ch, :] - b_mid))
                viol = jnp.maximum(viol, jnp.max(ends, axis=0, keepdims=True))
                qe.append((q3 * jnp.exp2(rel)).astype(BF16).reshape(rb, dk))
                ke.append((k3 * jnp.exp2(-rel)).astype(BF16).reshape(rb, dk))
                qh.append((q3 * jnp.exp2(b3)).astype(BF16).reshape(rb, dk))
                kd.append((k3 * jnp.exp2(b_edge - b3)).astype(BF16).reshape(rb, dk))
                e = jnp.exp2(b_edge)
                decay[hh][d] += [e[c] for c in range(blk_chunks)]

            for i in range(rb // sup):
                sl = slice(i * sup, (i + 1) * sup)
                a = jnp.where(masks[0], _dot_nt(qe[0][sl], ke[0][sl]), 0.0) \
                    + jnp.where(masks[1], _dot_nt(qe[1][sl], ke[1][sl]), 0.0)
                oi_ref[hh, blk * rb + i * sup:blk * rb + (i + 1) * sup, :] = _dot(a.astype(BF16), vb[sl])
            kd2 = jnp.concatenate(kd, axis=1)
            gs[hh] += [_dot_tn(vb[c * ch:(c + 1) * ch], kd2[c * ch:(c + 1) * ch]) for c in range(blk_chunks)]
            qh2[hh].append(jnp.concatenate(qh, axis=1))

    o_inter = []
    for hh in heads:
        entering = [[None] * n_chunks, [None] * n_chunks]
        for d in range(2):
            state = jnp.zeros((HGRN_DV, dk), F32)
            order = range(n_chunks - 1, -1, -1) if d == 1 else range(n_chunks)
            for c in order:
                entering[d][c] = state
                state = state * decay[hh][d][c] + gs[hh][c][:, d * dk:(d + 1) * dk]
        inter = []
        for c in range(n_chunks):
            st = jnp.concatenate([entering[0][c], entering[1][c]], axis=1).astype(BF16)
            lc = c % blk_chunks
            inter.append(_dot_nt(qh2[hh][c // blk_chunks][lc * ch:(lc + 1) * ch], st))
        o_inter.append(jnp.concatenate(inter, axis=0))

    @pl.when(jnp.max(viol) > HGRN_SAFE_EXP2)
    def _():
        def one_row(t, carry):
            t0 = pl.multiple_of((t // ch) * ch, ch)
            s_idx = t0 + lax.broadcasted_iota(jnp.int32, (ch, 1), 0)
            for hh in heads:
                acc = jnp.zeros((1, HGRN_DV), F32)
                for d in range(2):
                    bc = b_ref[hh, d, pl.ds(t0, ch), :]
                    bt = b_ref[hh, d, pl.ds(t, 1), :]
                    valid = (s_idx >= t) if d == 1 else (s_idx <= t)
                    dec = jnp.where(valid, jnp.exp2(jnp.minimum(bt - bc, 0.0)), 0.0)
                    wgt = dec * (q_ref[hh, pl.ds(t, 1), :] * k_ref[hh, d, pl.ds(t0, ch), :])
                    sc = jnp.sum(wgt, axis=1, keepdims=True)
                    acc = acc + jnp.sum(sc * v_ref[hh, pl.ds(t0, ch), :], axis=0, keepdims=True)
                oi_ref[hh, pl.ds(t, 1), :] = acc
            return carry

        lax.fori_loop(0, seq, one_row, 0)

    for hh in heads:
        o = o_inter[hh] + oi_ref[hh]
        ms = jnp.mean(o * o, axis=-1, keepdims=True)
        out_ref[0, :, hh * HGRN_DV:(hh + 1) * HGRN_DV] = \
            (o * lax.rsqrt(ms + NORM_EPS) * gw_ref[...] * gate_ref[hh]).astype(BF16)


def _hgrn(h, w_h, lb_params, gw, layer):
    b, s, d = h.shape
    hps = HGRN_HEADS_PER_STEP
    n_steps = w_h.shape[1] // (hps * HEAD_COLS)
    ndir, depth, _ = lb_params.shape
    return pl.pallas_call(
        functools.partial(_hgrn_kernel, layer),
        grid=(b, n_steps),
        in_specs=[
            pl.BlockSpec((1, s, d), lambda i, j: (i, 0, 0)),
            pl.BlockSpec((d, hps * HEAD_COLS), lambda i, j: (0, j)),
            pl.BlockSpec((ndir, depth, hps * HGRN_DK), lambda i, j: (0, 0, j)),
            pl.BlockSpec((1, HGRN_DV), lambda i, j: (0, 0)),
        ],
        out_specs=pl.BlockSpec((1, s, hps * HGRN_DV), lambda i, j: (i, 0, j)),
        out_shape=jax.ShapeDtypeStruct((b, s, n_steps * hps * HGRN_DV), BF16),
        scratch_shapes=[
            pltpu.VMEM((hps, s, HGRN_DK), F32),
            pltpu.VMEM((hps, s, HGRN_DV), F32),
            pltpu.VMEM((hps, 2, s, HGRN_DK), F32),
            pltpu.VMEM((hps, 2, s, HGRN_DK), F32),
            pltpu.VMEM((hps, s, HGRN_DV), F32),
            pltpu.VMEM((hps, s, HGRN_DV), F32),
        ],
        compiler_params=pltpu.CompilerParams(
            dimension_semantics=("arbitrary", "arbitrary"), vmem_limit_bytes=VMEM_LIMIT),
    )(h, w_h, lb_params, gw)


def _merge_kernel(last, x_ref, h_ref, ya_ref, yh_ref, wm_ref, wba_ref, wbh_ref, wo_ref, nw_ref, *out_refs):
    d = x_ref.shape[1]
    m = _dot(h_ref[...], wm_ref[...])
    merged = jax.nn.sigmoid(m[:, :d]) * _dot(ya_ref[...], wba_ref[...]) \
        + jax.nn.sigmoid(m[:, d:]) * _dot(yh_ref[...], wbh_ref[...])
    x_new = x_ref[...] + _dot(merged.astype(BF16), wo_ref[...])
    ms = jnp.mean(x_new * x_new, axis=-1, keepdims=True)
    normed = x_new * lax.rsqrt(ms + NORM_EPS) * nw_ref[...]
    if last:
        out_refs[0][...] = normed
    else:
        out_refs[0][...] = x_new
        out_refs[1][...] = normed.astype(BF16)


def _merge(x2, h2, ya2, yh2, wm, wba, wbh, wo, nw, last):
    n, d = x2.shape
    tm = MERGE_TILE
    tile = lambda i: (i, 0)
    whole = lambda i: (0, 0)
    if last:
        out_shape = (jax.ShapeDtypeStruct((n, d), F32),)
        out_specs = (pl.BlockSpec((tm, d), tile),)
    else:
        out_shape = (jax.ShapeDtypeStruct((n, d), F32), jax.ShapeDtypeStruct((n, d), BF16))
        out_specs = (pl.BlockSpec((tm, d), tile), pl.BlockSpec((tm, d), tile))
    return pl.pallas_call(
        functools.partial(_merge_kernel, last),
        grid=(n // tm,),
        in_specs=[
            pl.BlockSpec((tm, d), tile), pl.BlockSpec((tm, d), tile),
            pl.BlockSpec((tm, d), tile), pl.BlockSpec((tm, d), tile),
            pl.BlockSpec((d, 2 * d), whole), pl.BlockSpec((d, d), whole),
            pl.BlockSpec((d, d), whole), pl.BlockSpec((d, d), whole),
            pl.BlockSpec((1, d), whole),
        ],
        out_specs=out_specs,
        out_shape=out_shape,
        compiler_params=pltpu.CompilerParams(
            dimension_semantics=("arbitrary",), vmem_limit_bytes=VMEM_LIMIT),
    )(x2, h2, ya2, yh2, wm, wba, wbh, wo, nw.reshape(1, d))


def _rope_tables(seq):
    rows = seq // GRID_W
    row = jnp.repeat(jnp.arange(rows), GRID_W).astype(F32)
    col = jnp.tile(jnp.arange(GRID_W), rows).astype(F32)
    inv_freq = ROPE_THETA ** (-jnp.arange(0, ROPE_AXIS_DIM, 2, dtype=F32) / ROPE_AXIS_DIM)
    ang_r = row[:, None] * inv_freq
    ang_c = col[:, None] * inv_freq
    cr, sr, cc, sc = jnp.cos(ang_r), jnp.sin(ang_r), jnp.cos(ang_c), jnp.sin(ang_c)
    cos = jnp.concatenate([cr, cr, cc, cc], axis=1)
    sin = jnp.concatenate([-sr, sr, -sc, sc], axis=1)
    return jnp.tile(cos, (1, 2)), jnp.tile(sin, (1, 2)), cos.T, sin.T


def _layer_weights(w_l):
    aw = ATTN_HEADS * ATTN_HEAD_DIM
    kvw = ATTN_KV_HEADS * ATTN_HEAD_DIM
    hw = HGRN_HEADS * HGRN_DK
    o_q, o_k, o_v, o_ag = 0, aw, aw + kvw, aw + 2 * kvw
    o_h = o_ag + aw
    o_m = o_h + 5 * hw
    cols = []
    for g in range(ATTN_KV_HEADS):
        cols += [w_l[:, o_q + g * GROUP_Q:o_q + (g + 1) * GROUP_Q],
                 w_l[:, o_k + g * ATTN_HEAD_DIM:o_k + (g + 1) * ATTN_HEAD_DIM],
                 w_l[:, o_v + g * ATTN_HEAD_DIM:o_v + (g + 1) * ATTN_HEAD_DIM],
                 w_l[:, o_ag + g * GROUP_Q:o_ag + (g + 1) * GROUP_Q]]
    w_a = jnp.concatenate(cols, axis=1)
    cols = []
    for hh in range(HGRN_HEADS):
        cols += [w_l[:, o_h + part * hw + hh * HGRN_DK:o_h + part * hw + (hh + 1) * HGRN_DK] for part in range(5)]
    w_h = jnp.concatenate(cols, axis=1)
    return w_a, w_h, w_l[:, o_m:]


def kernel(x, w_in, norm_w, q_norm_w, k_norm_w, hgrn_lower_bounds, hgrn_norm_w, w_branch_attn, w_branch_hgrn,
           w_out, final_norm_w):
    b, s, d = x.shape
    depth = w_in.shape[0]
    assert s % Q_TILE == 0 and s % HGRN_CHUNK == 0 and s % GRID_W == 0 and (b * s) % MERGE_TILE == 0
    cos, sin, cos_t, sin_t = _rope_tables(s)
    x2 = x.reshape(b * s, d)
    w_in_b = w_in.astype(BF16)
    h2 = _prenorm(x2, norm_w[0])
    out = None
    for layer in range(depth):
        w_a, w_h, w_m = _layer_weights(w_in_b[layer])
        h3 = h2.reshape(b, s, d)
        qw = jnp.tile(q_norm_w[layer], ATTN_GROUP).reshape(1, GROUP_Q)
        kw = jnp.concatenate([k_norm_w[layer], jnp.ones((ATTN_HEAD_DIM,), F32)]).reshape(1, GROUP_KV)
        ya = _attention(h3, w_a, cos, sin, cos_t, sin_t, qw, kw)
        yh = _hgrn(h3, w_h, hgrn_lower_bounds.astype(F32), hgrn_norm_w[layer].reshape(1, HGRN_DV), layer)
        last = layer == depth - 1
        nw = final_norm_w if last else norm_w[layer + 1]
        res = _merge(x2, h2, ya.reshape(b * s, -1), yh.reshape(b * s, -1), w_m,
                     w_branch_attn[layer].astype(BF16), w_branch_hgrn[layer].astype(BF16),
                     w_out[layer].astype(BF16), nw, last)
        if last:
            out = res[0]
        else:
            x2, h2 = res
    return out.reshape(b, s, d)
```

```python
import functools
import math

import jax
import jax.numpy as jnp
import numpy as np
from jax import lax
from jax.experimental import pallas as pl
from jax.experimental.pallas import tpu as pltpu

F32 = jnp.float32
BF16 = jnp.bfloat16

GRID_W = 64
ATTN_HEADS = 16
ATTN_KV_HEADS = 4
ATTN_HEAD_DIM = 64
ATTN_GROUP = ATTN_HEADS // ATTN_KV_HEADS
ROPE_AXIS_DIM = ATTN_HEAD_DIM // 2
ROPE_THETA = 10000.0
HGRN_HEADS = 8
HGRN_DK = 128
HGRN_DV = 128
NORM_EPS = 1e-6

GROUP_Q = ATTN_GROUP * ATTN_HEAD_DIM
GROUP_KV = 2 * ATTN_HEAD_DIM
GROUP_COLS = GROUP_Q + GROUP_KV + GROUP_Q
HEAD_COLS = 5 * HGRN_DK

Q_TILE = 256
VT_ROWS = ATTN_HEAD_DIM + 16
KEY_BLOCK = 256
HGRN_CHUNK = 64
HGRN_MID = HGRN_CHUNK // 2 - 1
HGRN_SUPER = 256
PROJ_BLOCKS = 4
HGRN_HEADS_PER_STEP = 2
HGRN_SAFE_EXP2 = 90.0
LOG2_E = math.log2(math.e)
MERGE_TILE = 1024

VMEM_LIMIT = 56 * 1024 * 1024


def _dot(a, b):
    return jnp.dot(a, b, preferred_element_type=F32)


def _dot_nt(a, b):
    return lax.dot_general(a, b, (((1,), (1,)), ((), ())), preferred_element_type=F32)


def _dot_tn(a, b):
    return lax.dot_general(a, b, (((0,), (0,)), ((), ())), preferred_element_type=F32)


def _silu(x):
    return x * jax.nn.sigmoid(x)


def _split2(x):
    hi = x.astype(BF16)
    lo = (x - hi.astype(F32)).astype(BF16)
    return hi, lo


def _prenorm_kernel(x_ref, w_ref, h_ref):
    x = x_ref[...]
    ms = jnp.mean(x * x, axis=-1, keepdims=True)
    h_ref[...] = (x * lax.rsqrt(ms + NORM_EPS) * w_ref[...]).astype(BF16)


def _prenorm(x2, w):
    n, d = x2.shape
    tm = MERGE_TILE
    return pl.pallas_call(
        _prenorm_kernel,
        grid=(n // tm,),
        in_specs=[pl.BlockSpec((tm, d), lambda i: (i, 0)), pl.BlockSpec((1, d), lambda i: (0, 0))],
        out_specs=pl.BlockSpec((tm, d), lambda i: (i, 0)),
        out_shape=jax.ShapeDtypeStruct((n, d), BF16),
        compiler_params=pltpu.CompilerParams(dimension_semantics=("arbitrary",)),
    )(x2, w.reshape(1, d))


def _rope_partner(x):
    width = x.shape[-1]
    lane = lax.broadcasted_iota(jnp.int32, x.shape, 1)
    first = (lane & (ROPE_AXIS_DIM - 1)) < (ROPE_AXIS_DIM // 2)
    return jnp.where(first, pltpu.roll(x, width - ROPE_AXIS_DIM // 2, 1), pltpu.roll(x, ROPE_AXIS_DIM // 2, 1))


def _attn_kernel(h_ref, w_ref, cos_ref, sin_ref, cos_t_ref, sin_t_ref, qw_ref, kw_ref, out_ref,
                 qt_ref, k_ref, vt_ref, gate_ref, s_ref, m_ref):
    seq = h_ref.shape[1]
    n_tiles = seq // Q_TILE
    hd = ATTN_HEAD_DIM

    w = w_ref[...]
    q_scale = qw_ref[...] * (hd ** -0.5 * LOG2_E)
    r = lax.broadcasted_iota(jnp.int32, (GROUP_Q, GROUP_Q), 0)
    c = lax.broadcasted_iota(jnp.int32, (GROUP_Q, GROUP_Q), 1)
    same_head = jnp.where((r ^ c) < hd, 1.0, 0.0).astype(BF16)
    rb = seq // PROJ_BLOCKS
    is_k = lax.broadcasted_iota(jnp.int32, (rb, GROUP_KV), 1) < hd
    vt_ref[hd:, :] = jnp.ones((VT_ROWS - hd, seq), BF16)

    for blk in range(PROJ_BLOCKS):
        rows = slice(blk * rb, (blk + 1) * rb)
        proj = _dot(h_ref[0, rows, :], w)
        q = proj[:, :GROUP_Q]
        kv = proj[:, GROUP_Q:GROUP_Q + GROUP_KV]
        gate_ref[rows, :] = _silu(proj[:, GROUP_Q + GROUP_KV:])

        hi, lo = _split2(q * q)
        ssq = _dot(hi, same_head) + _dot(lo, same_head)
        qn_t = (q * lax.rsqrt(ssq * (1.0 / hd) + NORM_EPS) * q_scale).T
        half = ROPE_AXIS_DIM // 2
        parts = []
        for grp in range(GROUP_Q // ROPE_AXIS_DIM):
            base = grp * ROPE_AXIS_DIM
            parts += [qn_t[base + half:base + 2 * half], qn_t[base:base + half]]
        partner_t = jnp.concatenate(parts, axis=0)
        cos_t = jnp.concatenate([cos_t_ref[:, rows]] * ATTN_GROUP, axis=0)
        sin_t = jnp.concatenate([sin_t_ref[:, rows]] * ATTN_GROUP, axis=0)
        qr_t = (qn_t * cos_t + partner_t * sin_t).astype(BF16)
        for t in range(rb // Q_TILE):
            for j in range(ATTN_GROUP):
                qt_ref[blk * (rb // Q_TILE) + t, :, j * Q_TILE:(j + 1) * Q_TILE] = \
                    qr_t[j * hd:(j + 1) * hd, t * Q_TILE:(t + 1) * Q_TILE]

        cos = cos_ref[rows, :]
        sin = sin_ref[rows, :]
        ssk = jnp.sum(jnp.where(is_k, kv * kv, 0.0), axis=-1, keepdims=True)
        kn = kv * lax.rsqrt(ssk * (1.0 / hd) + NORM_EPS) * kw_ref[...]
        kr = kn * cos + _rope_partner(kn) * sin
        k_ref[rows, :] = kr[:, :hd].astype(BF16)
        vt_ref[:hd, rows] = kv.T[hd:, :].astype(BF16)

    def step(nxt, cur):
        if cur is not None:
            m = m_ref[cur[1]]
        acc = None
        m_new = None
        for kb in range(seq // KEY_BLOCK):
            rows = slice(kb * KEY_BLOCK, (kb + 1) * KEY_BLOCK)
            if nxt is not None:
                sc = _dot(k_ref[rows, :], qt_ref[nxt[0]])
                s_ref[nxt[1], rows, :] = sc
                cm = jnp.max(sc, axis=0, keepdims=True)
                m_new = cm if m_new is None else jnp.maximum(m_new, cm)
            if cur is not None:
                p = jnp.exp2(s_ref[cur[1], rows, :] - m).astype(BF16)
                part = _dot(vt_ref[:, rows], p)
                acc = part if acc is None else acc + part
        if nxt is not None:
            m_ref[nxt[1]] = m_new
        if cur is not None:
            o = acc[:hd] / acc[hd:hd + 1]
            o = jnp.concatenate([o[:, j * Q_TILE:(j + 1) * Q_TILE] for j in range(ATTN_GROUP)], axis=0).T
            col = pl.multiple_of(cur[0] * Q_TILE, Q_TILE)
            out_ref[0, pl.ds(col, Q_TILE), :] = (o * gate_ref[pl.ds(col, Q_TILE), :]).astype(BF16)

    step((0, 0), None)

    def pair(j, carry):
        i = 2 * j
        step((i + 1, 1), (i, 0))
        step((i + 2, 0), (i + 1, 1))
        return carry

    lax.fori_loop(0, n_tiles // 2 - 1, pair, 0)
    step((n_tiles - 1, 1), (n_tiles - 2, 0))
    step(None, (n_tiles - 1, 1))


def _attention(h, w_a, cos, sin, cos_t, sin_t, qw, kw):
    b, s, d = h.shape
    g = w_a.shape[1] // GROUP_COLS
    return pl.pallas_call(
        _attn_kernel,
        grid=(b, g),
        in_specs=[
            pl.BlockSpec((1, s, d), lambda i, j: (i, 0, 0)),
            pl.BlockSpec((d, GROUP_COLS), lambda i, j: (0, j)),
            pl.BlockSpec((s, GROUP_KV), lambda i, j: (0, 0)),
            pl.BlockSpec((s, GROUP_KV), lambda i, j: (0, 0)),
            pl.BlockSpec((ATTN_HEAD_DIM, s), lambda i, j: (0, 0)),
            pl.BlockSpec((ATTN_HEAD_DIM, s), lambda i, j: (0, 0)),
            pl.BlockSpec((1, GROUP_Q), lambda i, j: (0, 0)),
            pl.BlockSpec((1, GROUP_KV), lambda i, j: (0, 0)),
        ],
        out_specs=pl.BlockSpec((1, s, GROUP_Q), lambda i, j: (i, 0, j)),
        out_shape=jax.ShapeDtypeStruct((b, s, g * GROUP_Q), BF16),
        scratch_shapes=[
            pltpu.VMEM((s // Q_TILE, ATTN_HEAD_DIM, ATTN_GROUP * Q_TILE), BF16),
            pltpu.VMEM((s, ATTN_HEAD_DIM), BF16),
            pltpu.VMEM((VT_ROWS, s), BF16),
            pltpu.VMEM((s, GROUP_Q), F32),
            pltpu.VMEM((2, s, ATTN_GROUP * Q_TILE), F32),
            pltpu.VMEM((2, 1, ATTN_GROUP * Q_TILE), F32),
        ],
        compiler_params=pltpu.CompilerParams(
            dimension_semantics=("arbitrary", "arbitrary"), vmem_limit_bytes=VMEM_LIMIT),
    )(h, w_a, cos, sin, cos_t, sin_t, qw, kw)


def _forget_gate(z, lb):
    u = jnp.exp(-jnp.abs(z))
    t = 1.0 + u
    log_sig = jnp.minimum(z, 0.0) - jnp.log(t)
    sig_neg = jnp.where(z >= 0.0, u, 1.0) / t
    a = jnp.log(lb)
    c = jnp.log(1.0 - lb) + log_sig
    mx = jnp.maximum(a, c)
    g = mx + jnp.log(1.0 + jnp.exp(jnp.minimum(a, c) - mx))
    return g * LOG2_E, (1.0 - lb) * sig_neg


def _hgrn_kernel(layer, h_ref, w_ref, lbp_ref, gw_ref, out_ref, q_ref, v_ref, k_ref, b_ref, oi_ref, gate_ref):
    seq = h_ref.shape[1]
    ch = HGRN_CHUNK
    sup = HGRN_SUPER
    n_chunks = seq // ch
    dk = HGRN_DK
    rb = seq // PROJ_BLOCKS
    blk_chunks = rb // ch
    heads = range(HGRN_HEADS_PER_STEP)

    row = lax.broadcasted_iota(jnp.int32, (sup, sup), 0)
    colm = lax.broadcasted_iota(jnp.int32, (sup, sup), 1)
    same_chunk = (row ^ colm) < ch
    masks = (same_chunk & (colm <= row), same_chunk & (colm >= row))
    tri = [jnp.where(mk, 1.0, 0.0).astype(BF16) for mk in masks]

    lbs = []
    for d in range(2):
        p = lbp_ref[d]
        depth = p.shape[0]
        pm = p[0:1]
        for i in range(1, depth):
            pm = jnp.maximum(pm, p[i:i + 1])
        ex = [jnp.exp(p[i:i + 1] - pm) for i in range(depth)]
        tot = ex[0]
        for i in range(1, depth):
            tot = tot + ex[i]
        first = ex[0] / tot
        cum = first
        for i in range(1, layer + 1):
            cum = cum + ex[i] / tot
        lbs.append(cum - first)

    w = w_ref[...]
    decay = [[[], []] for _ in heads]
    gs = [[] for _ in heads]
    qh2 = [[] for _ in heads]
    viol = jnp.zeros((1, 1, dk), F32)
    for blk in range(PROJ_BLOCKS):
        rows = slice(blk * rb, (blk + 1) * rb)
        proj_all = _dot(h_ref[0, rows, :], w)
        for hh in heads:
            proj = proj_all[:, hh * HEAD_COLS:(hh + 1) * HEAD_COLS]
            q = _silu(proj[:, :dk]) * (dk ** -0.5)
            v = proj[:, 3 * dk:4 * dk]
            gate_ref[hh, rows, :] = _silu(proj[:, 4 * dk:])
            q_ref[hh, rows, :] = q
            v_ref[hh, rows, :] = v
            vb = v.astype(BF16)
            q3 = q.reshape(blk_chunks, ch, dk)

            qe, ke, qh, kd = [], [], [], []
            for d in range(2):
                backward = d == 1
                g, k = _forget_gate(proj[:, (1 + d) * dk:(2 + d) * dk],
                                    lbs[d][:, hh * dk:(hh + 1) * dk])
                k_ref[hh, d, rows, :] = k
                g2 = jnp.concatenate(_split2(g), axis=1)
                bs = []
                for i in range(rb // sup):
                    cs = _dot(tri[d], g2[i * sup:(i + 1) * sup])
                    bs.append(cs[:, :dk] + cs[:, dk:])
                b = jnp.concatenate(bs, axis=0)
                b_ref[hh, d, rows, :] = b

                edge = 0 if backward else ch - 1
                mid = ch - 1 - HGRN_MID if backward else HGRN_MID
                b3 = b.reshape(blk_chunks, ch, dk)
                k3 = k.reshape(blk_chunks, ch, dk)
                b_edge = b3[:, edge:edge + 1, :]
                b_mid = b3[:, mid:mid + 1, :]
                rel = b3 - b_mid
                ends = jnp.maximum(jnp.abs(b3[:, 0:1, :] - b_mid), jnp.abs(b3[:, ch - 1:ch, :] - b_mid))
                viol = jnp.maximum(viol, jnp.max(ends, axis=0, keepdims=True))
                qe.append((q3 * jnp.exp2(rel)).astype(BF16).reshape(rb, dk))
                ke.append((k3 * jnp.exp2(-rel)).astype(BF16).reshape(rb, dk))
                qh.append((q3 * jnp.exp2(b3)).astype(BF16).reshape(rb, dk))
                kd.append((k3 * jnp.exp2(b_edge - b3)).astype(BF16).reshape(rb, dk))
                e = jnp.exp2(b_edge)
                decay[hh][d] += [e[c] for c in range(blk_chunks)]

            for i in range(rb // sup):
                sl = slice(i * sup, (i + 1) * sup)
                a = jnp.where(masks[0], _dot_nt(qe[0][sl], ke[0][sl]), 0.0) \
                    + jnp.where(masks[1], _dot_nt(qe[1][sl], ke[1][sl]), 0.0)
                oi_ref[hh, blk * rb + i * sup:blk * rb + (i + 1) * sup, :] = _dot(a.astype(BF16), vb[sl])
            kd2 = jnp.concatenate(kd, axis=1)
            gs[hh] += [_dot_tn(vb[c * ch:(c + 1) * ch], kd2[c * ch:(c + 1) * ch]) for c in range(blk_chunks)]
            qh2[hh].append(jnp.concatenate(qh, axis=1))

    o_inter = []
    for hh in heads:
        entering = [[None] * n_chunks, [None] * n_chunks]
        for d in range(2):
            state = jnp.zeros((HGRN_DV, dk), F32)
            order = range(n_chunks - 1, -1, -1) if d == 1 else range(n_chunks)
            for c in order:
                entering[d][c] = state
                state = state * decay[hh][d][c] + gs[hh][c][:, d * dk:(d + 1) * dk]
        inter = []
        for c in range(n_chunks):
            st = jnp.concatenate([entering[0][c], entering[1][c]], axis=1).astype(BF16)
            lc = c % blk_chunks
            inter.append(_dot_nt(qh2[hh][c // blk_chunks][lc * ch:(lc + 1) * ch], st))
        o_inter.append(jnp.concatenate(inter, axis=0))

    @pl.when(jnp.max(viol) > HGRN_SAFE_EXP2)
    def _():
        def one_row(t, carry):
            t0 = pl.multiple_of((t // ch) * ch, ch)
            s_idx = t0 + lax.broadcasted_iota(jnp.int32, (ch, 1), 0)
            for hh in heads:
                acc = jnp.zeros((1, HGRN_DV), F32)
                for d in range(2):
                    bc = b_ref[hh, d, pl.ds(t0, ch), :]
                    bt = b_ref[hh, d, pl.ds(t, 1), :]
                    valid = (s_idx >= t) if d == 1 else (s_idx <= t)
                    dec = jnp.where(valid, jnp.exp2(jnp.minimum(bt - bc, 0.0)), 0.0)
                    wgt = dec * (q_ref[hh, pl.ds(t, 1), :] * k_ref[hh, d, pl.ds(t0, ch), :])
                    sc = jnp.sum(wgt, axis=1, keepdims=True)
                    acc = acc + jnp.sum(sc * v_ref[hh, pl.ds(t0, ch), :], axis=0, keepdims=True)
                oi_ref[hh, pl.ds(t, 1), :] = acc
            return carry

        lax.fori_loop(0, seq, one_row, 0)

    for hh in heads:
        o = o_inter[hh] + oi_ref[hh]
        ms = jnp.mean(o * o, axis=-1, keepdims=True)
        out_ref[0, :, hh * HGRN_DV:(hh + 1) * HGRN_DV] = \
            (o * lax.rsqrt(ms + NORM_EPS) * gw_ref[...] * gate_ref[hh]).astype(BF16)


def _hgrn(h, w_h, lb_params, gw, layer):
    b, s, d = h.shape
    hps = HGRN_HEADS_PER_STEP
    n_steps = w_h.shape[1] // (hps * HEAD_COLS)
    ndir, depth, _ = lb_params.shape
    return pl.pallas_call(
        functools.partial(_hgrn_kernel, layer),
        grid=(b, n_steps),
        in_specs=[
            pl.BlockSpec((1, s, d), lambda i, j: (i, 0, 0)),
            pl.BlockSpec((d, hps * HEAD_COLS), lambda i, j: (0, j)),
            pl.BlockSpec((ndir, depth, hps * HGRN_DK), lambda i, j: (0, 0, j)),
            pl.BlockSpec((1, HGRN_DV), lambda i, j: (0, 0)),
        ],
        out_specs=pl.BlockSpec((1, s, hps * HGRN_DV), lambda i, j: (i, 0, j)),
        out_shape=jax.ShapeDtypeStruct((b, s, n_steps * hps * HGRN_DV), BF16),
        scratch_shapes=[
            pltpu.VMEM((hps, s, HGRN_DK), F32),
            pltpu.VMEM((hps, s, HGRN_DV), F32),
            pltpu.VMEM((hps, 2, s, HGRN_DK), F32),
            pltpu.VMEM((hps, 2, s, HGRN_DK), F32),
            pltpu.VMEM((hps, s, HGRN_DV), F32),
            pltpu.VMEM((hps, s, HGRN_DV), F32),
        ],
        compiler_params=pltpu.CompilerParams(
            dimension_semantics=("arbitrary", "arbitrary"), vmem_limit_bytes=VMEM_LIMIT),
    )(h, w_h, lb_params, gw)


def _merge_kernel(last, x_ref, h_ref, ya_ref, yh_ref, wm_ref, wba_ref, wbh_ref, wo_ref, nw_ref, *out_refs):
    d = x_ref.shape[1]
    m = _dot(h_ref[...], wm_ref[...])
    merged = jax.nn.sigmoid(m[:, :d]) * _dot(ya_ref[...], wba_ref[...]) \
        + jax.nn.sigmoid(m[:, d:]) * _dot(yh_ref[...], wbh_ref[...])
    x_new = x_ref[...] + _dot(merged.astype(BF16), wo_ref[...])
    ms = jnp.mean(x_new * x_new, axis=-1, keepdims=True)
    normed = x_new * lax.rsqrt(ms + NORM_EPS) * nw_ref[...]
    if last:
        out_refs[0][...] = normed
    else:
        out_refs[0][...] = x_new
        out_refs[1][...] = normed.astype(BF16)


def _merge(x2, h2, ya2, yh2, wm, wba, wbh, wo, nw, last):
    n, d = x2.shape
    tm = MERGE_TILE
    tile = lambda i: (i, 0)
    whole = lambda i: (0, 0)
    if last:
        out_shape = (jax.ShapeDtypeStruct((n, d), F32),)
        out_specs = (pl.BlockSpec((tm, d), tile),)
    else:
        out_shape = (jax.ShapeDtypeStruct((n, d), F32), jax.ShapeDtypeStruct((n, d), BF16))
        out_specs = (pl.BlockSpec((tm, d), tile), pl.BlockSpec((tm, d), tile))
    return pl.pallas_call(
        functools.partial(_merge_kernel, last),
        grid=(n // tm,),
        in_specs=[
            pl.BlockSpec((tm, d), tile), pl.BlockSpec((tm, d), tile),
            pl.BlockSpec((tm, d), tile), pl.BlockSpec((tm, d), tile),
            pl.BlockSpec((d, 2 * d), whole), pl.BlockSpec((d, d), whole),
            pl.BlockSpec((d, d), whole), pl.BlockSpec((d, d), whole),
            pl.BlockSpec((1, d), whole),
        ],
        out_specs=out_specs,
        out_shape=out_shape,
        compiler_params=pltpu.CompilerParams(
            dimension_semantics=("arbitrary",), vmem_limit_bytes=VMEM_LIMIT),
    )(x2, h2, ya2, yh2, wm, wba, wbh, wo, nw.reshape(1, d))


def _rope_tables(seq):
    pos = np.arange(seq)
    row = (pos // GRID_W).astype(np.float32)
    col = (pos % GRID_W).astype(np.float32)
    inv_freq = (ROPE_THETA ** (-np.arange(0, ROPE_AXIS_DIM, 2, dtype=np.float32) / ROPE_AXIS_DIM)).astype(np.float32)
    ang_r = row[:, None] * inv_freq
    ang_c = col[:, None] * inv_freq
    cr, sr, cc, sc = np.cos(ang_r), np.sin(ang_r), np.cos(ang_c), np.sin(ang_c)
    cos = np.concatenate([cr, cr, cc, cc], axis=1).astype(np.float32)
    sin = np.concatenate([-sr, sr, -sc, sc], axis=1).astype(np.float32)
    return (jnp.asarray(np.tile(cos, (1, 2))), jnp.asarray(np.tile(sin, (1, 2))),
            jnp.asarray(np.ascontiguousarray(cos.T)), jnp.asarray(np.ascontiguousarray(sin.T)))


def _layer_weights(w_l):
    aw = ATTN_HEADS * ATTN_HEAD_DIM
    kvw = ATTN_KV_HEADS * ATTN_HEAD_DIM
    hw = HGRN_HEADS * HGRN_DK
    o_q, o_k, o_v, o_ag = 0, aw, aw + kvw, aw + 2 * kvw
    o_h = o_ag + aw
    o_m = o_h + 5 * hw
    d_model = w_l.shape[0]
    g = ATTN_KV_HEADS
    w_a = jnp.concatenate([
        w_l[:, o_q:o_k].reshape(d_model, g, GROUP_Q), w_l[:, o_k:o_v].reshape(d_model, g, ATTN_HEAD_DIM),
        w_l[:, o_v:o_ag].reshape(d_model, g, ATTN_HEAD_DIM), w_l[:, o_ag:o_h].reshape(d_model, g, GROUP_Q)],
        axis=2).reshape(d_model, g * GROUP_COLS)
    w_h = w_l[:, o_h:o_m].reshape(d_model, 5, HGRN_HEADS, HGRN_DK).transpose(0, 2, 1, 3)
    w_h = w_h.reshape(d_model, HGRN_HEADS * HEAD_COLS)
    return w_a, w_h, w_l[:, o_m:]


def kernel(x, w_in, norm_w, q_norm_w, k_norm_w, hgrn_lower_bounds, hgrn_norm_w, w_branch_attn, w_branch_hgrn,
           w_out, final_norm_w):
    b, s, d = x.shape
    depth = w_in.shape[0]
    assert s % Q_TILE == 0 and s % HGRN_CHUNK == 0 and s % GRID_W == 0 and (b * s) % MERGE_TILE == 0
    cos, sin, cos_t, sin_t = _rope_tables(s)
    x2 = x.reshape(b * s, d)
    w_in_b = w_in.astype(BF16)
    h2 = _prenorm(x2, norm_w[0])
    out = None
    for layer in range(depth):
        w_a, w_h, w_m = _layer_weights(w_in_b[layer])
        h3 = h2.reshape(b, s, d)
        qw = jnp.tile(q_norm_w[layer], ATTN_GROUP).reshape(1, GROUP_Q)
        kw = jnp.concatenate([k_norm_w[layer], jnp.ones((ATTN_HEAD_DIM,), F32)]).reshape(1, GROUP_KV)
        ya = _attention(h3, w_a, cos, sin, cos_t, sin_t, qw, kw)
        yh = _hgrn(h3, w_h, hgrn_lower_bounds.astype(F32), hgrn_norm_w[layer].reshape(1, HGRN_DV), layer)
        last = layer == depth - 1
        nw = final_norm_w if last else norm_w[layer + 1]
        res = _merge(x2, h2, ya.reshape(b * s, -1), yh.reshape(b * s, -1), w_m,
                     w_branch_attn[layer].astype(BF16), w_branch_hgrn[layer].astype(BF16),
                     w_out[layer].astype(BF16), nw, last)
        if last:
            out = res[0]
        else:
            x2, h2 = res
    return out.reshape(b, s, d)
```

```python
import functools
import math

import jax
import jax.numpy as jnp
import numpy as np
from jax import lax
from jax.experimental import pallas as pl
from jax.experimental.pallas import tpu as pltpu

F32 = jnp.float32
BF16 = jnp.bfloat16

GRID_W = 64
ATTN_HEADS = 16
ATTN_KV_HEADS = 4
ATTN_HEAD_DIM = 64
ATTN_GROUP = ATTN_HEADS // ATTN_KV_HEADS
ROPE_AXIS_DIM = ATTN_HEAD_DIM // 2
ROPE_THETA = 10000.0
HGRN_HEADS = 8
HGRN_DK = 128
HGRN_DV = 128
NORM_EPS = 1e-6

GROUP_Q = ATTN_GROUP * ATTN_HEAD_DIM
GROUP_KV = 2 * ATTN_HEAD_DIM
GROUP_COLS = GROUP_Q + GROUP_KV + GROUP_Q
HEAD_COLS = 5 * HGRN_DK

Q_TILE = 128
VT_ROWS = ATTN_HEAD_DIM + 16
KEY_BLOCK = 256
HGRN_CHUNK = 64
HGRN_MID = HGRN_CHUNK // 2 - 1
HGRN_SUPER = 256
PROJ_BLOCKS = 4
HGRN_HEADS_PER_STEP = 2
HGRN_SAFE_EXP2 = 90.0
LOG2_E = math.log2(math.e)
MERGE_TILE = 1024

VMEM_LIMIT = 56 * 1024 * 1024


def _dot(a, b):
    return jnp.dot(a, b, preferred_element_type=F32)


def _dot_nt(a, b):
    return lax.dot_general(a, b, (((1,), (1,)), ((), ())), preferred_element_type=F32)


def _dot_tn(a, b):
    return lax.dot_general(a, b, (((0,), (0,)), ((), ())), preferred_element_type=F32)


def _silu(x):
    return x * jax.nn.sigmoid(x)


def _split2(x):
    hi = x.astype(BF16)
    lo = (x - hi.astype(F32)).astype(BF16)
    return hi, lo


def _prenorm_kernel(x_ref, w_ref, h_ref):
    x = x_ref[...]
    ms = jnp.mean(x * x, axis=-1, keepdims=True)
    h_ref[...] = (x * lax.rsqrt(ms + NORM_EPS) * w_ref[...]).astype(BF16)


def _prenorm(x2, w):
    n, d = x2.shape
    tm = MERGE_TILE
    return pl.pallas_call(
        _prenorm_kernel,
        grid=(n // tm,),
        in_specs=[pl.BlockSpec((tm, d), lambda i: (i, 0)), pl.BlockSpec((1, d), lambda i: (0, 0))],
        out_specs=pl.BlockSpec((tm, d), lambda i: (i, 0)),
        out_shape=jax.ShapeDtypeStruct((n, d), BF16),
        compiler_params=pltpu.CompilerParams(dimension_semantics=("arbitrary",)),
    )(x2, w.reshape(1, d))


def _rope_partner(x):
    width = x.shape[-1]
    lane = lax.broadcasted_iota(jnp.int32, x.shape, 1)
    first = (lane & (ROPE_AXIS_DIM - 1)) < (ROPE_AXIS_DIM // 2)
    return jnp.where(first, pltpu.roll(x, width - ROPE_AXIS_DIM // 2, 1), pltpu.roll(x, ROPE_AXIS_DIM // 2, 1))


def _attn_kernel(h_ref, w_ref, cos_ref, sin_ref, cos_t_ref, sin_t_ref, qw_ref, kw_ref, out_ref,
                 qt_ref, k_ref, vt_ref, gate_ref, s_ref, m_ref):
    seq = h_ref.shape[1]
    n_tiles = seq // Q_TILE
    hd = ATTN_HEAD_DIM

    w = w_ref[...]
    q_scale = qw_ref[...] * (hd ** -0.5 * LOG2_E)
    r = lax.broadcasted_iota(jnp.int32, (GROUP_Q, GROUP_Q), 0)
    c = lax.broadcasted_iota(jnp.int32, (GROUP_Q, GROUP_Q), 1)
    same_head = jnp.where((r ^ c) < hd, 1.0, 0.0).astype(BF16)
    rb = seq // PROJ_BLOCKS
    is_k = lax.broadcasted_iota(jnp.int32, (rb, GROUP_KV), 1) < hd
    vt_ref[hd:, :] = jnp.ones((VT_ROWS - hd, seq), BF16)

    for blk in range(PROJ_BLOCKS):
        rows = slice(blk * rb, (blk + 1) * rb)
        proj = _dot(h_ref[0, rows, :], w)
        q = proj[:, :GROUP_Q]
        kv = proj[:, GROUP_Q:GROUP_Q + GROUP_KV]
        gate_ref[rows, :] = _silu(proj[:, GROUP_Q + GROUP_KV:])

        hi, lo = _split2(q * q)
        ssq = _dot(hi, same_head) + _dot(lo, same_head)
        qn_t = (q * lax.rsqrt(ssq * (1.0 / hd) + NORM_EPS) * q_scale).T
        half = ROPE_AXIS_DIM // 2
        parts = []
        for grp in range(GROUP_Q // ROPE_AXIS_DIM):
            base = grp * ROPE_AXIS_DIM
            parts += [qn_t[base + half:base + 2 * half], qn_t[base:base + half]]
        partner_t = jnp.concatenate(parts, axis=0)
        cos_t = jnp.concatenate([cos_t_ref[:, rows]] * ATTN_GROUP, axis=0)
        sin_t = jnp.concatenate([sin_t_ref[:, rows]] * ATTN_GROUP, axis=0)
        qr_t = (qn_t * cos_t + partner_t * sin_t).astype(BF16)
        for t in range(rb // Q_TILE):
            for j in range(ATTN_GROUP):
                qt_ref[blk * (rb // Q_TILE) + t, :, j * Q_TILE:(j + 1) * Q_TILE] = \
                    qr_t[j * hd:(j + 1) * hd, t * Q_TILE:(t + 1) * Q_TILE]

        cos = cos_ref[rows, :]
        sin = sin_ref[rows, :]
        ssk = jnp.sum(jnp.where(is_k, kv * kv, 0.0), axis=-1, keepdims=True)
        kn = kv * lax.rsqrt(ssk * (1.0 / hd) + NORM_EPS) * kw_ref[...]
        kr = kn * cos + _rope_partner(kn) * sin
        k_ref[rows, :] = kr[:, :hd].astype(BF16)
        vt_ref[:hd, rows] = kv.T[hd:, :].astype(BF16)

    def step(nxt, cur):
        if cur is not None:
            m = m_ref[cur[1]]
        acc = None
        m_new = None
        for kb in range(seq // KEY_BLOCK):
            rows = slice(kb * KEY_BLOCK, (kb + 1) * KEY_BLOCK)
            if nxt is not None:
                sc = _dot(k_ref[rows, :], qt_ref[nxt[0]])
                s_ref[nxt[1], rows, :] = sc
                cm = jnp.max(sc, axis=0, keepdims=True)
                m_new = cm if m_new is None else jnp.maximum(m_new, cm)
            if cur is not None:
                p = jnp.exp2(s_ref[cur[1], rows, :] - m).astype(BF16)
                part = _dot(vt_ref[:, rows], p)
                acc = part if acc is None else acc + part
        if nxt is not None:
            m_ref[nxt[1]] = m_new
        if cur is not None:
            o = acc[:hd] / acc[hd:hd + 1]
            o = jnp.concatenate([o[:, j * Q_TILE:(j + 1) * Q_TILE] for j in range(ATTN_GROUP)], axis=0).T
            col = pl.multiple_of(cur[0] * Q_TILE, Q_TILE)
            out_ref[0, pl.ds(col, Q_TILE), :] = (o * gate_ref[pl.ds(col, Q_TILE), :]).astype(BF16)

    step((0, 0), None)

    def pair(j, carry):
        i = 2 * j
        step((i + 1, 1), (i, 0))
        step((i + 2, 0), (i + 1, 1))
        return carry

    lax.fori_loop(0, n_tiles // 2 - 1, pair, 0)
    step((n_tiles - 1, 1), (n_tiles - 2, 0))
    step(None, (n_tiles - 1, 1))


def _attention(h, w_a, cos, sin, cos_t, sin_t, qw, kw):
    b, s, d = h.shape
    g = w_a.shape[1] // GROUP_COLS
    return pl.pallas_call(
        _attn_kernel,
        grid=(b, g),
        in_specs=[
            pl.BlockSpec((1, s, d), lambda i, j: (i, 0, 0)),
            pl.BlockSpec((d, GROUP_COLS), lambda i, j: (0, j)),
            pl.BlockSpec((s, GROUP_KV), lambda i, j: (0, 0)),
            pl.BlockSpec((s, GROUP_KV), lambda i, j: (0, 0)),
            pl.BlockSpec((ATTN_HEAD_DIM, s), lambda i, j: (0, 0)),
            pl.BlockSpec((ATTN_HEAD_DIM, s), lambda i, j: (0, 0)),
            pl.BlockSpec((1, GROUP_Q), lambda i, j: (0, 0)),
            pl.BlockSpec((1, GROUP_KV), lambda i, j: (0, 0)),
        ],
        out_specs=pl.BlockSpec((1, s, GROUP_Q), lambda i, j: (i, 0, j)),
        out_shape=jax.ShapeDtypeStruct((b, s, g * GROUP_Q), BF16),
        scratch_shapes=[
            pltpu.VMEM((s // Q_TILE, ATTN_HEAD_DIM, ATTN_GROUP * Q_TILE), BF16),
            pltpu.VMEM((s, ATTN_HEAD_DIM), BF16),
            pltpu.VMEM((VT_ROWS, s), BF16),
            pltpu.VMEM((s, GROUP_Q), F32),
            pltpu.VMEM((2, s, ATTN_GROUP * Q_TILE), F32),
            pltpu.VMEM((2, 1, ATTN_GROUP * Q_TILE), F32),
        ],
        compiler_params=pltpu.CompilerParams(
            dimension_semantics=("arbitrary", "arbitrary"), vmem_limit_bytes=VMEM_LIMIT),
    )(h, w_a, cos, sin, cos_t, sin_t, qw, kw)


def _forget_gate(z, lb):
    u = jnp.exp(-jnp.abs(z))
    t = 1.0 + u
    log_sig = jnp.minimum(z, 0.0) - jnp.log(t)
    sig_neg = jnp.where(z >= 0.0, u, 1.0) / t
    a = jnp.log(lb)
    c = jnp.log(1.0 - lb) + log_sig
    mx = jnp.maximum(a, c)
    g = mx + jnp.log(1.0 + jnp.exp(jnp.minimum(a, c) - mx))
    return g * LOG2_E, (1.0 - lb) * sig_neg


def _hgrn_kernel(layer, h_ref, w_ref, lbp_ref, gw_ref, out_ref, q_ref, v_ref, k_ref, b_ref, oi_ref, gate_ref):
    seq = h_ref.shape[1]
    ch = HGRN_CHUNK
    sup = HGRN_SUPER
    n_chunks = seq // ch
    dk = HGRN_DK
    rb = seq // PROJ_BLOCKS
    blk_chunks = rb // ch
    heads = range(HGRN_HEADS_PER_STEP)

    row = lax.broadcasted_iota(jnp.int32, (sup, sup), 0)
    colm = lax.broadcasted_iota(jnp.int32, (sup, sup), 1)
    same_chunk = (row ^ colm) < ch
    masks = (same_chunk & (colm <= row), same_chunk & (colm >= row))
    tri = [jnp.where(mk, 1.0, 0.0).astype(BF16) for mk in masks]

    lbs = []
    for d in range(2):
        p = lbp_ref[d]
        depth = p.shape[0]
        pm = p[0:1]
        for i in range(1, depth):
            pm = jnp.maximum(pm, p[i:i + 1])
        ex = [jnp.exp(p[i:i + 1] - pm) for i in range(depth)]
        tot = ex[0]
        for i in range(1, depth):
            tot = tot + ex[i]
        first = ex[0] / tot
        cum = first
        for i in range(1, layer + 1):
            cum = cum + ex[i] / tot
        lbs.append(cum - first)

    w = w_ref[...]
    decay = [[[], []] for _ in heads]
    gs = [[] for _ in heads]
    qh2 = [[] for _ in heads]
    viol = jnp.zeros((1, 1, dk), F32)
    for blk in range(PROJ_BLOCKS):
        rows = slice(blk * rb, (blk + 1) * rb)
        proj_all = _dot(h_ref[0, rows, :], w)
        for hh in heads:
            proj = proj_all[:, hh * HEAD_COLS:(hh + 1) * HEAD_COLS]
            q = _silu(proj[:, :dk]) * (dk ** -0.5)
            v = proj[:, 3 * dk:4 * dk]
            gate_ref[hh, rows, :] = _silu(proj[:, 4 * dk:])
            q_ref[hh, rows, :] = q
            v_ref[hh, rows, :] = v
            vb = v.astype(BF16)
            q3 = q.reshape(blk_chunks, ch, dk)

            qe, ke, qh, kd = [], [], [], []
            for d in range(2):
                backward = d == 1
                g, k = _forget_gate(proj[:, (1 + d) * dk:(2 + d) * dk],
                                    lbs[d][:, hh * dk:(hh + 1) * dk])
                k_ref[hh, d, rows, :] = k
                g2 = jnp.concatenate(_split2(g), axis=1)
                bs = []
                for i in range(rb // sup):
                    cs = _dot(tri[d], g2[i * sup:(i + 1) * sup])
                    bs.append(cs[:, :dk] + cs[:, dk:])
                b = jnp.concatenate(bs, axis=0)
                b_ref[hh, d, rows, :] = b

                edge = 0 if backward else ch - 1
                mid = ch - 1 - HGRN_MID if backward else HGRN_MID
                b3 = b.reshape(blk_chunks, ch, dk)
                k3 = k.reshape(blk_chunks, ch, dk)
                b_edge = b3[:, edge:edge + 1, :]
                b_mid = b3[:, mid:mid + 1, :]
                rel = b3 - b_mid
                ends = jnp.maximum(jnp.abs(b3[:, 0:1, :] - b_mid), jnp.abs(b3[:, ch - 1:ch, :] - b_mid))
                viol = jnp.maximum(viol, jnp.max(ends, axis=0, keepdims=True))
                qe.append((q3 * jnp.exp2(rel)).astype(BF16).reshape(rb, dk))
                ke.append((k3 * jnp.exp2(-rel)).astype(BF16).reshape(rb, dk))
                qh.append((q3 * jnp.exp2(b3)).astype(BF16).reshape(rb, dk))
                kd.append((k3 * jnp.exp2(b_edge - b3)).astype(BF16).reshape(rb, dk))
                e = jnp.exp2(b_edge)
                decay[hh][d] += [e[c] for c in range(blk_chunks)]

            for i in range(rb // sup):
                sl = slice(i * sup, (i + 1) * sup)
                a = jnp.where(masks[0], _dot_nt(qe[0][sl], ke[0][sl]), 0.0) \
                    + jnp.where(masks[1], _dot_nt(qe[1][sl], ke[1][sl]), 0.0)
                oi_ref[hh, blk * rb + i * sup:blk * rb + (i + 1) * sup, :] = _dot(a.astype(BF16), vb[sl])
            kd2 = jnp.concatenate(kd, axis=1)
            gs[hh] += [_dot_tn(vb[c * ch:(c + 1) * ch], kd2[c * ch:(c + 1) * ch]) for c in range(blk_chunks)]
            qh2[hh].append(jnp.concatenate(qh, axis=1))

    o_inter = []
    for hh in heads:
        entering = [[None] * n_chunks, [None] * n_chunks]
        for d in range(2):
            state = jnp.zeros((HGRN_DV, dk), F32)
            order = range(n_chunks - 1, -1, -1) if d == 1 else range(n_chunks)
            for c in order:
                entering[d][c] = state
                state = state * decay[hh][d][c] + gs[hh][c][:, d * dk:(d + 1) * dk]
        inter = []
        for c in range(n_chunks):
            st = jnp.concatenate([entering[0][c], entering[1][c]], axis=1).astype(BF16)
            lc = c % blk_chunks
            inter.append(_dot_nt(qh2[hh][c // blk_chunks][lc * ch:(lc + 1) * ch], st))
        o_inter.append(jnp.concatenate(inter, axis=0))

    @pl.when(jnp.max(viol) > HGRN_SAFE_EXP2)
    def _():
        def one_row(t, carry):
            t0 = pl.multiple_of((t // ch) * ch, ch)
            s_idx = t0 + lax.broadcasted_iota(jnp.int32, (ch, 1), 0)
            for hh in heads:
                acc = jnp.zeros((1, HGRN_DV), F32)
                for d in range(2):
                    bc = b_ref[hh, d, pl.ds(t0, ch), :]
                    bt = b_ref[hh, d, pl.ds(t, 1), :]
                    valid = (s_idx >= t) if d == 1 else (s_idx <= t)
                    dec = jnp.where(valid, jnp.exp2(jnp.minimum(bt - bc, 0.0)), 0.0)
                    wgt = dec * (q_ref[hh, pl.ds(t, 1), :] * k_ref[hh, d, pl.ds(t0, ch), :])
                    sc = jnp.sum(wgt, axis=1, keepdims=True)
                    acc = acc + jnp.sum(sc * v_ref[hh, pl.ds(t0, ch), :], axis=0, keepdims=True)
                oi_ref[hh, pl.ds(t, 1), :] = acc
            return carry

        lax.fori_loop(0, seq, one_row, 0)

    for hh in heads:
        o = o_inter[hh] + oi_ref[hh]
        ms = jnp.mean(o * o, axis=-1, keepdims=True)
        out_ref[0, :, hh * HGRN_DV:(hh + 1) * HGRN_DV] = \
            (o * lax.rsqrt(ms + NORM_EPS) * gw_ref[...] * gate_ref[hh]).astype(BF16)


def _hgrn(h, w_h, lb_params, gw, layer):
    b, s, d = h.shape
    hps = HGRN_HEADS_PER_STEP
    n_steps = w_h.shape[1] // (hps * HEAD_COLS)
    ndir, depth, _ = lb_params.shape
    return pl.pallas_call(
        functools.partial(_hgrn_kernel, layer),
        grid=(b, n_steps),
        in_specs=[
            pl.BlockSpec((1, s, d), lambda i, j: (i, 0, 0)),
            pl.BlockSpec((d, hps * HEAD_COLS), lambda i, j: (0, j)),
            pl.BlockSpec((ndir, depth, hps * HGRN_DK), lambda i, j: (0, 0, j)),
            pl.BlockSpec((1, HGRN_DV), lambda i, j: (0, 0)),
        ],
        out_specs=pl.BlockSpec((1, s, hps * HGRN_DV), lambda i, j: (i, 0, j)),
        out_shape=jax.ShapeDtypeStruct((b, s, n_steps * hps * HGRN_DV), BF16),
        scratch_shapes=[
            pltpu.VMEM((hps, s, HGRN_DK), F32),
            pltpu.VMEM((hps, s, HGRN_DV), F32),
            pltpu.VMEM((hps, 2, s, HGRN_DK), F32),
            pltpu.VMEM((hps, 2, s, HGRN_DK), F32),
            pltpu.VMEM((hps, s, HGRN_DV), F32),
            pltpu.VMEM((hps, s, HGRN_DV), F32),
        ],
        compiler_params=pltpu.CompilerParams(
            dimension_semantics=("arbitrary", "arbitrary"), vmem_limit_bytes=VMEM_LIMIT),
    )(h, w_h, lb_params, gw)


def _merge_kernel(last, x_ref, h_ref, ya_ref, yh_ref, wm_ref, wba_ref, wbh_ref, wo_ref, nw_ref, *out_refs):
    d = x_ref.shape[1]
    m = _dot(h_ref[...], wm_ref[...])
    merged = jax.nn.sigmoid(m[:, :d]) * _dot(ya_ref[...], wba_ref[...]) \
        + jax.nn.sigmoid(m[:, d:]) * _dot(yh_ref[...], wbh_ref[...])
    x_new = x_ref[...] + _dot(merged.astype(BF16), wo_ref[...])
    ms = jnp.mean(x_new * x_new, axis=-1, keepdims=True)
    normed = x_new * lax.rsqrt(ms + NORM_EPS) * nw_ref[...]
    if last:
        out_refs[0][...] = normed
    else:
        out_refs[0][...] = x_new
        out_refs[1][...] = normed.astype(BF16)


def _merge(x2, h2, ya2, yh2, wm, wba, wbh, wo, nw, last):
    n, d = x2.shape
    tm = MERGE_TILE
    tile = lambda i: (i, 0)
    whole = lambda i: (0, 0)
    if last:
        out_shape = (jax.ShapeDtypeStruct((n, d), F32),)
        out_specs = (pl.BlockSpec((tm, d), tile),)
    else:
        out_shape = (jax.ShapeDtypeStruct((n, d), F32), jax.ShapeDtypeStruct((n, d), BF16))
        out_specs = (pl.BlockSpec((tm, d), tile), pl.BlockSpec((tm, d), tile))
    return pl.pallas_call(
        functools.partial(_merge_kernel, last),
        grid=(n // tm,),
        in_specs=[
            pl.BlockSpec((tm, d), tile), pl.BlockSpec((tm, d), tile),
            pl.BlockSpec((tm, d), tile), pl.BlockSpec((tm, d), tile),
            pl.BlockSpec((d, 2 * d), whole), pl.BlockSpec((d, d), whole),
            pl.BlockSpec((d, d), whole), pl.BlockSpec((d, d), whole),
            pl.BlockSpec((1, d), whole),
        ],
        out_specs=out_specs,
        out_shape=out_shape,
        compiler_params=pltpu.CompilerParams(
            dimension_semantics=("arbitrary",), vmem_limit_bytes=VMEM_LIMIT),
    )(x2, h2, ya2, yh2, wm, wba, wbh, wo, nw.reshape(1, d))


def _rope_tables(seq):
    pos = np.arange(seq)
    row = (pos // GRID_W).astype(np.float32)
    col = (pos % GRID_W).astype(np.float32)
    inv_freq = (ROPE_THETA ** (-np.arange(0, ROPE_AXIS_DIM, 2, dtype=np.float32) / ROPE_AXIS_DIM)).astype(np.float32)
    ang_r = row[:, None] * inv_freq
    ang_c = col[:, None] * inv_freq
    cr, sr, cc, sc = np.cos(ang_r), np.sin(ang_r), np.cos(ang_c), np.sin(ang_c)
    cos = np.concatenate([cr, cr, cc, cc], axis=1).astype(np.float32)
    sin = np.concatenate([-sr, sr, -sc, sc], axis=1).astype(np.float32)
    return (jnp.asarray(np.tile(cos, (1, 2))), jnp.asarray(np.tile(sin, (1, 2))),
            jnp.asarray(np.ascontiguousarray(cos.T)), jnp.asarray(np.ascontiguousarray(sin.T)))


def _layer_weights(w_l):
    aw = ATTN_HEADS * ATTN_HEAD_DIM
    kvw = ATTN_KV_HEADS * ATTN_HEAD_DIM
    hw = HGRN_HEADS * HGRN_DK
    o_q, o_k, o_v, o_ag = 0, aw, aw + kvw, aw + 2 * kvw
    o_h = o_ag + aw
    o_m = o_h + 5 * hw
    d_model = w_l.shape[0]
    g = ATTN_KV_HEADS
    w_a = jnp.concatenate([
        w_l[:, o_q:o_k].reshape(d_model, g, GROUP_Q), w_l[:, o_k:o_v].reshape(d_model, g, ATTN_HEAD_DIM),
        w_l[:, o_v:o_ag].reshape(d_model, g, ATTN_HEAD_DIM), w_l[:, o_ag:o_h].reshape(d_model, g, GROUP_Q)],
        axis=2).reshape(d_model, g * GROUP_COLS)
    w_h = w_l[:, o_h:o_m].reshape(d_model, 5, HGRN_HEADS, HGRN_DK).transpose(0, 2, 1, 3)
    w_h = w_h.reshape(d_model, HGRN_HEADS * HEAD_COLS)
    return w_a, w_h, w_l[:, o_m:]


def kernel(x, w_in, norm_w, q_norm_w, k_norm_w, hgrn_lower_bounds, hgrn_norm_w, w_branch_attn, w_branch_hgrn,
           w_out, final_norm_w):
    b, s, d = x.shape
    depth = w_in.shape[0]
    assert s % Q_TILE == 0 and s % HGRN_CHUNK == 0 and s % GRID_W == 0 and (b * s) % MERGE_TILE == 0
    cos, sin, cos_t, sin_t = _rope_tables(s)
    x2 = x.reshape(b * s, d)
    w_in_b = w_in.astype(BF16)
    h2 = _prenorm(x2, norm_w[0])
    out = None
    for layer in range(depth):
        w_a, w_h, w_m = _layer_weights(w_in_b[layer])
        h3 = h2.reshape(b, s, d)
        qw = jnp.tile(q_norm_w[layer], ATTN_GROUP).reshape(1, GROUP_Q)
        kw = jnp.concatenate([k_norm_w[layer], jnp.ones((ATTN_HEAD_DIM,), F32)]).reshape(1, GROUP_KV)
        ya = _attention(h3, w_a, cos, sin, cos_t, sin_t, qw, kw)
        yh = _hgrn(h3, w_h, hgrn_lower_bounds.astype(F32), hgrn_norm_w[layer].reshape(1, HGRN_DV), layer)
        last = layer == depth - 1
        nw = final_norm_w if last else norm_w[layer + 1]
        res = _merge(x2, h2, ya.reshape(b * s, -1), yh.reshape(b * s, -1), w_m,
                     w_branch_attn[layer].astype(BF16), w_branch_hgrn[layer].astype(BF16),
                     w_out[layer].astype(BF16), nw, last)
        if last:
            out = res[0]
        else:
            x2, h2 = res
    return out.reshape(b, s, d)
```
